```python
import jax, jax.numpy as jnp
from jax import lax
import numpy as np

D_MODEL = 4096
BATCH = 4
SEQ = 4096
DEPTH = 1

H_A = 16
DH_A = 128
H_B = 32
HKV_B = 4
G_B = H_B // HKV_B
DH_B = 64
WINDOW = 128
NUM_BUCKETS = 32
MAX_DISTANCE = 128
BLOCK = 128
D_FF = ((8 * D_MODEL // 3 + 255) // 256) * 256
EPS = 1e-6

W_QA = H_A * DH_A
W_KA = H_A * DH_A
W_VA = H_A * DH_A
W_FA = H_A
W_QB = H_B * DH_B
W_KB = HKV_B * DH_B
W_VB = HKV_B * DH_B
W_GA = D_MODEL
W_GB = D_MODEL
W_IN = W_QA + W_KA + W_VA + W_FA + W_QB + W_KB + W_VB + W_GA + W_GB

kernel_name = "fox_swa_sink_gated_hybrid_block"


def rms_norm(x, g):
    xf = x.astype(jnp.float32)
    y = xf * lax.rsqrt(jnp.mean(xf * xf, axis=-1, keepdims=True) + EPS)
    return (y * g.astype(jnp.float32)).astype(x.dtype)


def t5_bucket(dist):
    max_exact = NUM_BUCKETS // 2
    small = dist < max_exact
    large = max_exact + (np.log(np.maximum(dist, 1) / max_exact) / np.log(MAX_DISTANCE / max_exact)
                         * (NUM_BUCKETS - max_exact)).astype(np.int64)
    large = np.minimum(large, NUM_BUCKETS - 1)
    return np.where(small, dist, large)


def band_geometry(n_blocks):
    ql = np.arange(BLOCK)[:, None]
    kl = np.arange(2 * BLOCK)[None, :]
    dist = ql + BLOCK - kl
    in_window = (dist >= 0) & (dist < WINDOW)
    key_global = np.arange(n_blocks)[:, None, None] * BLOCK - BLOCK + kl[None]
    mask = in_window[None] & (key_global >= 0)
    bucket = t5_bucket(np.clip(dist, 0, None))
    return jnp.asarray(mask), jnp.asarray(bucket.astype(np.int32))


def forgetting_attention(q, k, v, f_logit):
    B, S, H, D = q.shape
    n_blocks = S // BLOCK
    log_f = jax.nn.log_sigmoid(f_logit.astype(jnp.float32))
    c = lax.cumsum(log_f, axis=1).transpose(0, 2, 1)
    key_pos = jnp.arange(S)
    scale = D ** -0.5

    def one_block(i):
        start = i * BLOCK
        q_blk = lax.dynamic_slice_in_dim(q, start, BLOCK, axis=1)
        c_q = lax.dynamic_slice_in_dim(c, start, BLOCK, axis=2)
        s = jnp.einsum('bqhd,bkhd->bhqk', q_blk, k, preferred_element_type=jnp.float32) * scale
        s = s + c_q[..., None] - c[:, :, None, :]
        q_pos = start + jnp.arange(BLOCK)
        causal = key_pos[None, :] <= q_pos[:, None]
        p = jax.nn.softmax(jnp.where(causal, s, -jnp.inf), axis=-1)
        return jnp.einsum('bhqk,bkhd->bqhd', p.astype(v.dtype), v)

    out = lax.map(one_block, jnp.arange(n_blocks))
    return out.transpose(1, 0, 2, 3, 4).reshape(B, S, H * D)


def sliding_window_sink_attention(q, k, v, sinks, rel_bias):
    B, S = q.shape[:2]
    n_blocks = S // BLOCK
    mask, bucket = band_geometry(n_blocks)
    bias = rel_bias.astype(jnp.float32)[bucket].transpose(2, 0, 1).reshape(HKV_B, G_B, BLOCK, 2 * BLOCK)
    qb = q.reshape(B, n_blocks, BLOCK, HKV_B, G_B, DH_B)

    def band(t):
        tp = jnp.pad(t, ((0, 0), (BLOCK, 0), (0, 0), (0, 0))).reshape(B, n_blocks + 1, BLOCK, HKV_B, DH_B)
        return jnp.concatenate([tp[:, :-1], tp[:, 1:]], axis=2)

    kb, vb = band(k), band(v)
    s = jnp.einsum('bnqhgd,bnkhd->bnhgqk', qb, kb, preferred_element_type=jnp.float32) * (DH_B ** -0.5)
    s = jnp.where(mask[None, :, None, None], s + bias, -jnp.inf)
    sink = sinks.astype(jnp.float32).reshape(1, 1, HKV_B, G_B, 1, 1)
    m = jnp.maximum(jnp.max(s, axis=-1, keepdims=True), sink)
    p = jnp.exp(s - m)
    p = p / (jnp.sum(p, axis=-1, keepdims=True) + jnp.exp(sink - m))
    o = jnp.einsum('bnhgqk,bnkhd->bnqhgd', p.astype(v.dtype), vb)
    return o.reshape(B, S, H_B * DH_B)


def setup_inputs(seed: int = 0) -> dict:
    key = jax.random.key(seed)
    ks = jax.random.split(key, 16)
    f32 = jnp.float32

    def w(k, shape, fan_in):
        return jax.random.normal(k, shape, f32) * (fan_in ** -0.5)

    return {
        "x": jax.random.normal(ks[0], (BATCH, SEQ, D_MODEL), f32),
        "norm1_g": 1.0 + 0.02 * jax.random.normal(ks[1], (DEPTH, D_MODEL), f32),
        "w_in": w(ks[2], (DEPTH, D_MODEL, W_IN), D_MODEL),
        "b_forget": 0.1 * jax.random.normal(ks[3], (DEPTH, H_A), f32),
        "attn_sinks": 0.5 * jax.random.normal(ks[4], (DEPTH, H_B), f32),
        "rel_bias": 0.1 * jax.random.normal(ks[5], (NUM_BUCKETS, H_B), f32),
        "w_branch_a": w(ks[6], (DEPTH, H_A * DH_A, D_MODEL), H_A * DH_A),
        "w_branch_b": w(ks[7], (DEPTH, H_B * DH_B, D_MODEL), H_B * DH_B),
        "w_out": w(ks[8], (DEPTH, D_MODEL, D_MODEL), D_MODEL),
        "norm2_g": 1.0 + 0.02 * jax.random.normal(ks[9], (DEPTH, D_MODEL), f32),
        "w_ffn_gate": w(ks[10], (DEPTH, D_MODEL, D_FF), D_MODEL),
        "w_ffn_up": w(ks[11], (DEPTH, D_MODEL, D_FF), D_MODEL),
        "w_ffn_down": w(ks[12], (DEPTH, D_FF, D_MODEL), D_FF),
        "final_g": 1.0 + 0.02 * jax.random.normal(ks[13], (D_MODEL,), f32),
    }


def reference(x, norm1_g, w_in, b_forget, attn_sinks, rel_bias, w_branch_a, w_branch_b,
              w_out, norm2_g, w_ffn_gate, w_ffn_up, w_ffn_down, final_g):
    B, S, _ = x.shape
    split_at = list(np.cumsum([W_QA, W_KA, W_VA, W_FA, W_QB, W_KB, W_VB, W_GA])[:])
    for l in range(DEPTH):
        h = rms_norm(x, norm1_g[l])
        proj = jnp.einsum('bsd,dn->bsn', h, w_in[l])
        qa, ka, va, fa, qb, kb, vb, ga, gb = jnp.split(proj, split_at, axis=-1)
        qa = qa.reshape(B, S, H_A, DH_A)
        ka = ka.reshape(B, S, H_A, DH_A)
        va = va.reshape(B, S, H_A, DH_A)
        fa = fa + b_forget[l]
        qb = qb.reshape(B, S, HKV_B, G_B, DH_B)
        kb = kb.reshape(B, S, HKV_B, DH_B)
        vb = vb.reshape(B, S, HKV_B, DH_B)

        ya = jnp.einsum('bsc,cd->bsd', forgetting_attention(qa, ka, va, fa), w_branch_a[l])
        yb = jnp.einsum('bsc,cd->bsd', sliding_window_sink_attention(qb, kb, vb, attn_sinks[l], rel_bias),
                        w_branch_b[l])
        mixed = jax.nn.sigmoid(ga) * ya + jax.nn.sigmoid(gb) * yb
        x = x + jnp.einsum('bsd,de->bse', mixed, w_out[l])

        h = rms_norm(x, norm2_g[l])
        hidden = jax.nn.silu(jnp.einsum('bsd,df->bsf', h, w_ffn_gate[l])) * jnp.einsum('bsd,df->bsf', h, w_ffn_up[l])
        x = x + jnp.einsum('bsf,fd->bsd', hidden, w_ffn_down[l])
    return rms_norm(x, final_g)
```

```python
import functools

import numpy as np
import jax
import jax.numpy as jnp
from jax import lax
from jax.experimental import pallas as pl
from jax.experimental.pallas import tpu as pltpu

F32 = jnp.float32
BF16 = jnp.bfloat16

EPS = 1e-6
WINDOW = 128
BLOCK = 128
MAX_DISTANCE = 128
LANES = 128
MIB = 1024 * 1024


def _cparams(semantics, vmem_mib):
    return pltpu.CompilerParams(dimension_semantics=semantics,
                                vmem_limit_bytes=vmem_mib * MIB)


def _rmsnorm_kernel(x_ref, g_ref, o_ref):
    x = x_ref[...].astype(F32)
    ms = jnp.mean(x * x, axis=-1, keepdims=True)
    o_ref[...] = (x * lax.rsqrt(ms + EPS) * g_ref[...]).astype(o_ref.dtype)


def _rmsnorm(x, g, out_dtype, name, tm=256):
    m, d = x.shape
    return pl.pallas_call(
        _rmsnorm_kernel,
        grid=(m // tm,),
        in_specs=[pl.BlockSpec((tm, d), lambda i: (i, 0)),
                  pl.BlockSpec((1, d), lambda i: (0, 0))],
        out_specs=pl.BlockSpec((tm, d), lambda i: (i, 0)),
        out_shape=jax.ShapeDtypeStruct((m, d), out_dtype),
        compiler_params=_cparams(("parallel",), 40),
        name=name,
    )(x, g.reshape(1, d).astype(F32))


def _mm_kernel(*refs, n_a, n_b, pairs, n_extra, epilogue):
    a_refs = refs[:n_a]
    b_refs = refs[n_a:n_a + n_b]
    e_refs = refs[n_a + n_b:n_a + n_b + n_extra]
    o_ref = refs[-1]
    accs = [jnp.dot(a_refs[ai][...], b_refs[bi][...], preferred_element_type=F32)
            for ai, bi in pairs]
    o_ref[...] = epilogue(accs, [e[...] for e in e_refs]).astype(o_ref.dtype)


def _matmul(a_list, b_list, pairs, extras, epilogue, out_dtype, bm, bn, *,
            n_outer=False, vmem_mib=48, name):
    m = a_list[0].shape[0]
    n = b_list[0].shape[1]
    assert m % bm == 0 and n % bn == 0
    if n_outer:
        grid = (n // bn, m // bm)
        mi = lambda j, i: i
        nj = lambda j, i: j
    else:
        grid = (m // bm, n // bn)
        mi = lambda i, j: i
        nj = lambda i, j: j
    in_specs = []
    for a in a_list:
        assert a.shape[0] == m
        in_specs.append(pl.BlockSpec((bm, a.shape[1]), lambda *g: (mi(*g), 0)))
    for b in b_list:
        assert b.shape[1] == n
        if n_outer:
            spec = pl.BlockSpec((b.shape[0], bn), lambda *g: (0, nj(*g)),
                                pipeline_mode=pl.Buffered(1))
        else:
            spec = pl.BlockSpec((b.shape[0], bn), lambda *g: (0, nj(*g)))
        in_specs.append(spec)
    for e in extras:
        if e.shape[0] == 1:
            in_specs.append(pl.BlockSpec((1, bn), lambda *g: (0, nj(*g))))
        else:
            assert e.shape == (m, n)
            in_specs.append(pl.BlockSpec((bm, bn), lambda *g: (mi(*g), nj(*g))))
    kern = functools.partial(_mm_kernel, n_a=len(a_list), n_b=len(b_list), pairs=tuple(pairs),
                             n_extra=len(extras), epilogue=epilogue)
    return pl.pallas_call(
        kern,
        grid=grid,
        in_specs=in_specs,
        out_specs=pl.BlockSpec((bm, bn), lambda *g: (mi(*g), nj(*g))),
        out_shape=jax.ShapeDtypeStruct((m, n), out_dtype),
        compiler_params=_cparams(("parallel", "arbitrary"), vmem_mib),
        name=name,
    )(*a_list, *b_list, *extras)


def _ep_identity(accs, extras):
    return accs[0]


def _ep_residual(accs, extras):
    return extras[0] + accs[0]


def _ep_gated_merge(accs, extras):
    ga, gb, ya, yb = accs
    return jax.nn.sigmoid(ga) * ya + jax.nn.sigmoid(gb) * yb


def _ep_swiglu(accs, extras):
    g, u = accs
    return (g * jax.nn.sigmoid(g)) * u


def _cumlog_kernel(h_ref, wf_ref, bf_ref, ccol_ref, crow_ref, carry_ref, *, ts, n_heads):
    t = pl.program_id(1)

    @pl.when(t == 0)
    def _():
        carry_ref[...] = jnp.zeros_like(carry_ref)

    f = jnp.dot(h_ref[...], wf_ref[...], preferred_element_type=F32) + bf_ref[...]
    log_f = jnp.minimum(f, 0.0) - jnp.log(1.0 + jnp.exp(-jnp.abs(f)))
    row = lax.broadcasted_iota(jnp.int32, (ts, ts), 0)
    col = lax.broadcasted_iota(jnp.int32, (ts, ts), 1)
    tri = (col <= row).astype(F32)
    c = jnp.dot(tri, log_f, precision=lax.Precision.HIGHEST,
                preferred_element_type=F32) + carry_ref[...]
    carry_ref[...] = c[ts - 1:ts, :]
    crow_ref[0] = c.T[:n_heads, :]
    for h in range(n_heads):
        ccol_ref[0, h] = jnp.broadcast_to(c[:, h:h + 1], (ts, LANES))


def _cumlog(h1, w_f, b_f, batch, seq, n_heads, ts=512):
    d = h1.shape[1]
    nt = seq // ts
    return pl.pallas_call(
        functools.partial(_cumlog_kernel, ts=ts, n_heads=n_heads),
        grid=(batch, nt),
        in_specs=[pl.BlockSpec((ts, d), lambda b, t: (b * nt + t, 0)),
                  pl.BlockSpec((d, LANES), lambda b, t: (0, 0)),
                  pl.BlockSpec((1, LANES), lambda b, t: (0, 0))],
        out_specs=[pl.BlockSpec((1, n_heads, ts, LANES), lambda b, t: (b, 0, t, 0)),
                   pl.BlockSpec((1, n_heads, ts), lambda b, t: (b, 0, t))],
        out_shape=[jax.ShapeDtypeStruct((batch, n_heads, seq, LANES), F32),
                   jax.ShapeDtypeStruct((batch, n_heads, seq), F32)],
        scratch_shapes=[pltpu.VMEM((1, LANES), F32)],
        compiler_params=_cparams(("parallel", "arbitrary"), 40),
        name="fox_cumlog",
    )(h1, w_f, b_f)


def _fox_kernel(q_ref, k_ref, v_ref, cq_ref, ck_ref, o_ref, *, blk, n_blk, scale):
    h = pl.program_id(1)
    rep = blk // LANES
    row = lax.broadcasted_iota(jnp.int32, (blk, blk), 0)
    col = lax.broadcasted_iota(jnp.int32, (blk, blk), 1)
    causal = col <= row

    def q_block(i, _):
        q0 = pl.multiple_of(i * blk, blk)
        q = q_ref[pl.ds(q0, blk), :]
        cq = cq_ref[0, 0, pl.ds(q0, blk), :]
        cq = jnp.concatenate([cq] * rep, axis=1)

        def kv_step(j, carry, masked):
            m, l, acc = carry
            k0 = pl.multiple_of(j * blk, blk)
            k = k_ref[pl.ds(k0, blk), :]
            v = v_ref[pl.ds(k0, blk), :]
            s = lax.dot_general(q, k, (((1,), (1,)), ((), ())),
                                preferred_element_type=F32) * scale
            ck = ck_ref[0, pl.ds(h, 1), pl.ds(k0, blk)]
            s = s + cq - ck
            if masked:
                s = jnp.where(causal, s, -jnp.inf)
            m_new = jnp.maximum(m, jnp.max(s, axis=-1, keepdims=True))
            alpha = jnp.exp(m - m_new)
            p = jnp.exp(s - m_new)
            l = alpha * l + jnp.sum(p, axis=-1, keepdims=True)
            acc = alpha * acc + jnp.dot(p.astype(v.dtype), v, preferred_element_type=F32)
            return m_new, l, acc

        init = (jnp.full((blk, 1), -jnp.inf, F32), jnp.zeros((blk, 1), F32),
                jnp.zeros((blk, q.shape[1]), F32))
        carry = lax.fori_loop(0, i, lambda j, c: kv_step(j, c, False), init)
        m, l, acc = kv_step(i, carry, True)
        o_ref[pl.ds(q0, blk), :] = (acc / l).astype(o_ref.dtype)
        return 0

    lax.fori_loop(0, n_blk, q_block, 0)


def _fox_attention(qkv, ccol, crow, batch, seq, n_heads, dh, blk=256):
    assert dh == LANES
    n_blk = seq // blk
    kern = functools.partial(_fox_kernel, blk=blk, n_blk=n_blk, scale=dh ** -0.5)
    return pl.pallas_call(
        kern,
        grid=(batch, n_heads),
        in_specs=[pl.BlockSpec((seq, dh), lambda b, h: (b, h)),
                  pl.BlockSpec((seq, dh), lambda b, h: (b, n_heads + h)),
                  pl.BlockSpec((seq, dh), lambda b, h: (b, 2 * n_heads + h)),
                  pl.BlockSpec((1, 1, seq, LANES), lambda b, h: (b, h, 0, 0)),
                  pl.BlockSpec((1, n_heads, seq), lambda b, h: (b, 0, 0))],
        out_specs=pl.BlockSpec((seq, dh), lambda b, h: (b, h)),
        out_shape=jax.ShapeDtypeStruct((batch * seq, n_heads * dh), BF16),
        compiler_params=_cparams(("parallel", "arbitrary"), 40),
        name="fox_attention",
    )(qkv, qkv, qkv, ccol, crow)


def _t5_bucket(dist, num_buckets):
    max_exact = num_buckets // 2
    small = dist < max_exact
    large = max_exact + (np.log(np.maximum(dist, 1) / max_exact) / np.log(MAX_DISTANCE / max_exact)
                         * (num_buckets - max_exact)).astype(np.int64)
    large = np.minimum(large, num_buckets - 1)
    return np.where(small, dist, large)


def _bias_table_kernel(bucket_ref, rb_ref, o_ref, *, num_buckets):
    h = pl.program_id(0)
    bkt = bucket_ref[...]
    bias = jnp.zeros(bkt.shape, F32)
    for b in range(num_buckets):
        bias = jnp.where(bkt == b, rb_ref[b, h], bias)
    row = lax.broadcasted_iota(jnp.int32, bkt.shape, 0)
    col = lax.broadcasted_iota(jnp.int32, bkt.shape, 1)
    dist = row + BLOCK - col
    in_window = (dist >= 0) & (dist < WINDOW)
    o_ref[0] = jnp.where(in_window, bias, -jnp.inf)


def _bias_table(rel_bias, n_heads):
    num_buckets = rel_bias.shape[0]
    ql = np.arange(BLOCK)[:, None]
    kl = np.arange(2 * BLOCK)[None, :]
    bucket = _t5_bucket(np.clip(ql + BLOCK - kl, 0, None), num_buckets).astype(np.int32)
    return pl.pallas_call(
        functools.partial(_bias_table_kernel, num_buckets=num_buckets),
        grid=(n_heads,),
        in_specs=[pl.BlockSpec((BLOCK, 2 * BLOCK), lambda h: (0, 0)),
                  pl.BlockSpec(memory_space=pltpu.SMEM)],
        out_specs=pl.BlockSpec((1, BLOCK, 2 * BLOCK), lambda h: (h, 0, 0)),
        out_shape=jax.ShapeDtypeStruct((n_heads, BLOCK, 2 * BLOCK), F32),
        compiler_params=_cparams(("arbitrary",), 32),
        name="swa_bias_table",
    )(jnp.asarray(bucket), rel_bias.astype(F32))


def _swa_kernel(q_ref, kp_ref, kc_ref, vp_ref, vc_ref, bias_ref, sink_ref, o_ref, *,
                n_kv, groups, dh, scale):
    i = pl.program_id(1)
    k_band = jnp.concatenate([kp_ref[...], kc_ref[...]], axis=0)
    v_band = jnp.concatenate([vp_ref[...], vc_ref[...]], axis=0)
    col = lax.broadcasted_iota(jnp.int32, (BLOCK, 2 * BLOCK), 1)
    key_ok = (col >= BLOCK) | (i > 0)
    outs = []
    for hk in range(n_kv):
        k = k_band[:, hk * dh:(hk + 1) * dh]
        v = v_band[:, hk * dh:(hk + 1) * dh]
        for g in range(groups):
            hq = hk * groups + g
            q = q_ref[:, hq * dh:(hq + 1) * dh]
            s = lax.dot_general(q, k, (((1,), (1,)), ((), ())),
                                preferred_element_type=F32) * scale
            s = jnp.where(key_ok, s + bias_ref[hq], -jnp.inf)
            sink = sink_ref[hq]
            m = jnp.maximum(jnp.max(s, axis=-1, keepdims=True), sink)
            p = jnp.exp(s - m)
            denom = jnp.sum(p, axis=-1, keepdims=True) + jnp.exp(sink - m)
            o = jnp.dot(p.astype(v.dtype), v, preferred_element_type=F32)
            outs.append(o / denom)
    o_ref[...] = jnp.concatenate(outs, axis=1).astype(o_ref.dtype)


def _swa_attention(qkv, bias, sinks, batch, seq, n_heads, n_kv, dh, q_col0):
    nb = seq // BLOCK
    qw = n_heads * dh
    kw = n_kv * dh
    assert q_col0 % qw == 0 and (q_col0 + qw) % kw == 0
    qb = q_col0 // qw
    kb = (q_col0 + qw) // kw
    vb = kb + 1
    prev = lambda b, i: b * nb + jnp.maximum(i - 1, 0)
    cur = lambda b, i: b * nb + i
    kern = functools.partial(_swa_kernel, n_kv=n_kv, groups=n_heads // n_kv, dh=dh,
                             scale=dh ** -0.5)
    return pl.pallas_call(
        kern,
        grid=(batch, nb),
        in_specs=[pl.BlockSpec((BLOCK, qw), lambda b, i: (cur(b, i), qb)),
                  pl.BlockSpec((BLOCK, kw), lambda b, i: (prev(b, i), kb)),
                  pl.BlockSpec((BLOCK, kw), lambda b, i: (cur(b, i), kb)),
                  pl.BlockSpec((BLOCK, kw), lambda b, i: (prev(b, i), vb)),
                  pl.BlockSpec((BLOCK, kw), lambda b, i: (cur(b, i), vb)),
                  pl.BlockSpec((n_heads, BLOCK, 2 * BLOCK), lambda b, i: (0, 0, 0)),
                  pl.BlockSpec(memory_space=pltpu.SMEM)],
        out_specs=pl.BlockSpec((BLOCK, qw), lambda b, i: (cur(b, i), 0)),
        out_shape=jax.ShapeDtypeStruct((batch * seq, qw), BF16),
        compiler_params=_cparams(("parallel", "arbitrary"), 40),
        name="swa_attention",
    )(qkv, qkv, qkv, qkv, qkv, bias, sinks.astype(F32))


def kernel(x, norm1_g, w_in, b_forget, attn_sinks, rel_bias, w_branch_a, w_branch_b, w_out,
           norm2_g, w_ffn_gate, w_ffn_up, w_ffn_down, final_g):
    batch, seq, d = x.shape
    depth = w_in.shape[0]
    n_ha = b_forget.shape[1]
    n_hb = attn_sinks.shape[1]
    wa = w_branch_a.shape[1]
    wb = w_branch_b.shape[1]
    dh_a = wa // n_ha
    dh_b = wb // n_hb
    w_kvb = (w_in.shape[2] - 3 * wa - n_ha - wb - 2 * d) // 2
    n_kvb = w_kvb // dh_b
    c_f = 3 * wa
    c_qb = c_f + n_ha
    c_ga = c_qb + wb + 2 * w_kvb
    c_gb = c_ga + d

    xf = x.reshape(batch * seq, d)
    bias_tab = _bias_table(rel_bias, n_hb)

    for l in range(depth):
        w = w_in[l]
        w_qkv = jnp.concatenate([w[:, :c_f], w[:, c_qb:c_ga]], axis=1).astype(BF16)
        w_f = jnp.pad(w[:, c_f:c_qb], ((0, 0), (0, LANES - n_ha))).astype(BF16)
        b_f = jnp.pad(b_forget[l].astype(F32), (0, LANES - n_ha)).reshape(1, LANES)
        w_ga = w[:, c_ga:c_gb].astype(BF16)
        w_gb = w[:, c_gb:].astype(BF16)

        h1 = _rmsnorm(xf, norm1_g[l], BF16, "rmsnorm1")
        qkv = _matmul([h1], [w_qkv], [(0, 0)], [], _ep_identity, BF16, 1024, 512,
                      name="in_proj")
        ccol, crow = _cumlog(h1, w_f, b_f, batch, seq, n_ha)
        oa = _fox_attention(qkv, ccol, crow, batch, seq, n_ha, dh_a)
        ob = _swa_attention(qkv, bias_tab, attn_sinks[l], batch, seq, n_hb, n_kvb, dh_b, c_f)
        mixed = _matmul([h1, oa, ob],
                        [w_ga, w_gb, w_branch_a[l].astype(BF16), w_branch_b[l].astype(BF16)],
                        [(0, 0), (0, 1), (1, 2), (2, 3)], [], _ep_gated_merge, BF16,
                        1024, 256, vmem_mib=56, name="gated_merge")
        xf = _matmul([mixed], [w_out[l].astype(BF16)], [(0, 0)], [xf], _ep_residual, F32,
                     1024, 512, name="out_proj")

        h2 = _rmsnorm(xf, norm2_g[l], BF16, "rmsnorm2")
        hidden = _matmul([h2], [w_ffn_gate[l].astype(BF16), w_ffn_up[l].astype(BF16)],
                         [(0, 0), (0, 1)], [], _ep_swiglu, BF16, 1024, 256, name="ffn_gate_up")
        xf = _matmul([hidden], [w_ffn_down[l].astype(BF16)], [(0, 0)], [xf], _ep_residual, F32,
                     256, 1024, n_outer=True, vmem_mib=56, name="ffn_down")

    out = _rmsnorm(xf, final_g, x.dtype, "rmsnorm_final")
    return out.reshape(batch, seq, d)
```

```python
import functools

import numpy as np
import jax
import jax.numpy as jnp
from jax import lax
from jax.experimental import pallas as pl
from jax.experimental.pallas import tpu as pltpu

F32 = jnp.float32
BF16 = jnp.bfloat16

EPS = 1e-6
WINDOW = 128
BLOCK = 128
MAX_DISTANCE = 128
LANES = 128
MIB = 1024 * 1024
LOG2E = 1.4426950408889634


def _cparams(semantics, vmem_mib):
    return pltpu.CompilerParams(dimension_semantics=semantics,
                                vmem_limit_bytes=vmem_mib * MIB)


def _rmsnorm_kernel(x_ref, g_ref, o_ref):
    x = x_ref[...].astype(F32)
    ms = jnp.mean(x * x, axis=-1, keepdims=True)
    o_ref[...] = (x * lax.rsqrt(ms + EPS) * g_ref[...]).astype(o_ref.dtype)


def _rmsnorm(x, g, out_dtype, name, tm=256):
    m, d = x.shape
    return pl.pallas_call(
        _rmsnorm_kernel,
        grid=(m // tm,),
        in_specs=[pl.BlockSpec((tm, d), lambda i: (i, 0)),
                  pl.BlockSpec((1, d), lambda i: (0, 0))],
        out_specs=pl.BlockSpec((tm, d), lambda i: (i, 0)),
        out_shape=jax.ShapeDtypeStruct((m, d), out_dtype),
        compiler_params=_cparams(("parallel",), 40),
        name=name,
    )(x, g.reshape(1, d).astype(F32))


def _mm_kernel(*refs, n_a, n_b, pairs, n_extra, epilogue):
    a_refs = refs[:n_a]
    b_refs = refs[n_a:n_a + n_b]
    e_refs = refs[n_a + n_b:n_a + n_b + n_extra]
    o_ref = refs[-1]
    accs = [jnp.dot(a_refs[ai][...], b_refs[bi][...], preferred_element_type=F32)
            for ai, bi in pairs]
    o_ref[...] = epilogue(accs, [e[...] for e in e_refs]).astype(o_ref.dtype)


def _matmul(a_list, b_list, pairs, extras, epilogue, out_dtype, bm, bn, *,
            n_outer=False, vmem_mib=48, name):
    m = a_list[0].shape[0]
    n = b_list[0].shape[1]
    assert m % bm == 0 and n % bn == 0
    if n_outer:
        grid = (n // bn, m // bm)
        mi = lambda j, i: i
        nj = lambda j, i: j
    else:
        grid = (m // bm, n // bn)
        mi = lambda i, j: i
        nj = lambda i, j: j
    in_specs = []
    for a in a_list:
        assert a.shape[0] == m
        in_specs.append(pl.BlockSpec((bm, a.shape[1]), lambda *g: (mi(*g), 0)))
    for b in b_list:
        assert b.shape[1] == n
        if n_outer:
            spec = pl.BlockSpec((b.shape[0], bn), lambda *g: (0, nj(*g)),
                                pipeline_mode=pl.Buffered(1))
        else:
            spec = pl.BlockSpec((b.shape[0], bn), lambda *g: (0, nj(*g)))
        in_specs.append(spec)
    for e in extras:
        if e.shape[0] == 1:
            in_specs.append(pl.BlockSpec((1, bn), lambda *g: (0, nj(*g))))
        else:
            assert e.shape == (m, n)
            in_specs.append(pl.BlockSpec((bm, bn), lambda *g: (mi(*g), nj(*g))))
    kern = functools.partial(_mm_kernel, n_a=len(a_list), n_b=len(b_list), pairs=tuple(pairs),
                             n_extra=len(extras), epilogue=epilogue)
    return pl.pallas_call(
        kern,
        grid=grid,
        in_specs=in_specs,
        out_specs=pl.BlockSpec((bm, bn), lambda *g: (mi(*g), nj(*g))),
        out_shape=jax.ShapeDtypeStruct((m, n), out_dtype),
        compiler_params=_cparams(("parallel", "arbitrary"), vmem_mib),
        name=name,
    )(*a_list, *b_list, *extras)


def _ep_identity(accs, extras):
    return accs[0]


def _ep_residual(accs, extras):
    return extras[0] + accs[0]


def _ep_gated_merge(accs, extras):
    ga, gb, ya, yb = accs
    return jax.nn.sigmoid(ga) * ya + jax.nn.sigmoid(gb) * yb


def _ep_swiglu(accs, extras):
    g, u = accs
    return (g * jax.nn.sigmoid(g)) * u


N_SPLIT = 3


def _cumlog_kernel(h_ref, wf_ref, bf_ref, eq_ref, ek_ref, carry_ref, *, ts, n_heads, inv_scale):
    t = pl.program_id(1)

    @pl.when(t == 0)
    def _():
        carry_ref[...] = jnp.zeros_like(carry_ref)

    f = jnp.dot(h_ref[...], wf_ref[...], preferred_element_type=F32) + bf_ref[...]
    log_f = jnp.minimum(f, 0.0) - jnp.log(1.0 + jnp.exp(-jnp.abs(f)))
    row = lax.broadcasted_iota(jnp.int32, (ts, ts), 0)
    col = lax.broadcasted_iota(jnp.int32, (ts, ts), 1)
    tri = (col <= row).astype(F32)
    c = jnp.dot(tri, log_f, precision=lax.Precision.HIGHEST,
                preferred_element_type=F32) + carry_ref[...]
    carry_ref[...] = c[ts - 1:ts, :]
    lane = lax.broadcasted_iota(jnp.int32, (ts, LANES), 1)
    ones = jnp.where(lane < 2 * N_SPLIT, 1.0, 0.0)
    for h in range(n_heads):
        c3 = jnp.broadcast_to(c[:, h:h + 1] * inv_scale, (ts, LANES))
        hi = c3.astype(BF16).astype(F32)
        mid = (c3 - hi).astype(BF16).astype(F32)
        lo = (c3 - hi - mid).astype(BF16).astype(F32)
        eq = jnp.where(lane == 0, hi, jnp.where(lane == 1, mid, jnp.where(lane == 2, lo, ones)))
        ek = jnp.where(lane == N_SPLIT, -hi,
                       jnp.where(lane == N_SPLIT + 1, -mid,
                                 jnp.where(lane == N_SPLIT + 2, -lo, ones)))
        eq_ref[0, h] = eq.astype(BF16)
        ek_ref[0, h] = ek.astype(BF16)


def _cumlog(h1, w_f, b_f, batch, seq, n_heads, dh, ts=512):
    d = h1.shape[1]
    nt = seq // ts
    feat = pl.BlockSpec((1, n_heads, ts, LANES), lambda b, t: (b, 0, t, 0))
    return pl.pallas_call(
        functools.partial(_cumlog_kernel, ts=ts, n_heads=n_heads, inv_scale=dh ** 0.5),
        grid=(batch, nt),
        in_specs=[pl.BlockSpec((ts, d), lambda b, t: (b * nt + t, 0)),
                  pl.BlockSpec((d, LANES), lambda b, t: (0, 0)),
                  pl.BlockSpec((1, LANES), lambda b, t: (0, 0))],
        out_specs=[feat, feat],
        out_shape=[jax.ShapeDtypeStruct((batch, n_heads, seq, LANES), BF16)] * 2,
        scratch_shapes=[pltpu.VMEM((1, LANES), F32)],
        compiler_params=_cparams(("parallel", "arbitrary"), 40),
        name="fox_cumlog",
    )(h1, w_f, b_f)


FOX_BQ = 512
FOX_BK = 2 * FOX_BQ


def _fox_kernel(q_ref, k_ref, v_ref, eq_ref, ek_ref, o_ref, vt_ref, *, n_pairs, qk_scale):
    bq, bk = FOX_BQ, FOX_BK
    dh = q_ref.shape[1]

    for c in range(vt_ref.shape[0]):
        vt_ref[c] = v_ref[c * bq:(c + 1) * bq, :].T

    def tile(q, carry, c0, q0, width, masked):
        m, l, acc = carry
        k0 = pl.multiple_of(c0 * bq, bq)
        k = jnp.concatenate([k_ref[pl.ds(k0, width), :], ek_ref[0, 0, pl.ds(k0, width), :]], axis=1)
        s = lax.dot_general(k, q, (((1,), (1,)), ((), ())),
                            preferred_element_type=F32)
        if masked:
            key = k0 + lax.broadcasted_iota(jnp.int32, (width, bq), 0)
            qry = q0 + lax.broadcasted_iota(jnp.int32, (width, bq), 1)
            s = jnp.where(key <= qry, s, -jnp.inf)
        m_new = jnp.maximum(m, jnp.max(s, axis=0, keepdims=True))
        alpha = jnp.exp2((m - m_new) * qk_scale)
        p = jnp.exp2((s - m_new) * qk_scale)
        l = alpha * l + jnp.sum(p, axis=0, keepdims=True)
        p = p.astype(BF16)
        pv = jnp.dot(vt_ref[c0], p[:bq, :], preferred_element_type=F32)
        for t in range(1, width // bq):
            pv = pv + jnp.dot(vt_ref[c0 + t], p[t * bq:(t + 1) * bq, :],
                              preferred_element_type=F32)
        return m_new, l, alpha * acc + pv

    def pair(mi, _):
        i0 = 2 * mi
        qa0 = pl.multiple_of(i0 * bq, bq)
        qb0 = pl.multiple_of(i0 * bq + bq, bq)
        qa = jnp.concatenate([q_ref[pl.ds(qa0, bq), :], eq_ref[0, 0, pl.ds(qa0, bq), :]], axis=1)
        qb = jnp.concatenate([q_ref[pl.ds(qb0, bq), :], eq_ref[0, 0, pl.ds(qb0, bq), :]], axis=1)
        init = (jnp.full((1, bq), -jnp.inf, F32), jnp.zeros((1, bq), F32),
                jnp.zeros((dh, bq), F32))

        def chunk(c, carry):
            ca, cb = carry
            return (tile(qa, ca, 2 * c, qa0, bk, False), tile(qb, cb, 2 * c, qb0, bk, False))

        ca, cb = lax.fori_loop(0, mi, chunk, (init, init))
        _, la, acca = tile(qa, ca, i0, qa0, bq, True)
        _, lb, accb = tile(qb, cb, i0, qb0, bk, True)
        o_ref[pl.ds(qa0, bq), :] = (acca / la).T.astype(o_ref.dtype)
        o_ref[pl.ds(qb0, bq), :] = (accb / lb).T.astype(o_ref.dtype)
        return 0

    lax.fori_loop(0, n_pairs, pair, 0)


def _fox_attention(qkv, eq, ek, batch, seq, n_heads, dh):
    assert dh == LANES and seq % FOX_BK == 0
    nq = seq // FOX_BQ
    kern = functools.partial(_fox_kernel, n_pairs=nq // 2, qk_scale=dh ** -0.5 * LOG2E)
    feat = pl.BlockSpec((1, 1, seq, LANES), lambda b, h: (b, h, 0, 0))
    return pl.pallas_call(
        kern,
        grid=(batch, n_heads),
        in_specs=[pl.BlockSpec((seq, dh), lambda b, h: (b, h)),
                  pl.BlockSpec((seq, dh), lambda b, h: (b, n_heads + h)),
                  pl.BlockSpec((seq, dh), lambda b, h: (b, 2 * n_heads + h)),
                  feat, feat],
        out_specs=pl.BlockSpec((seq, dh), lambda b, h: (b, h)),
        out_shape=jax.ShapeDtypeStruct((batch * seq, n_heads * dh), BF16),
        scratch_shapes=[pltpu.VMEM((nq, dh, FOX_BQ), BF16)],
        compiler_params=_cparams(("parallel", "arbitrary"), 48),
        name="fox_attention",
    )(qkv, qkv, qkv, eq, ek)


def _t5_bucket(dist, num_buckets):
    max_exact = num_buckets // 2
    small = dist < max_exact
    large = max_exact + (np.log(np.maximum(dist, 1) / max_exact) / np.log(MAX_DISTANCE / max_exact)
                         * (num_buckets - max_exact)).astype(np.int64)
    large = np.minimum(large, num_buckets - 1)
    return np.where(small, dist, large)


def _bias_table_kernel(bucket_ref, rb_ref, o_ref, *, num_buckets):
    first = pl.program_id(0) == 0
    h = pl.program_id(1)
    bkt = bucket_ref[...]
    bias = jnp.zeros(bkt.shape, F32)
    for b in range(num_buckets):
        bias = jnp.where(bkt == b, rb_ref[b, h], bias)
    key = lax.broadcasted_iota(jnp.int32, bkt.shape, 0)
    qry = lax.broadcasted_iota(jnp.int32, bkt.shape, 1)
    dist = qry + BLOCK - key
    visible = (dist >= 0) & (dist < WINDOW) & ((key >= BLOCK) | jnp.logical_not(first))
    o_ref[0, 0] = jnp.where(visible, bias * LOG2E, -jnp.inf)


def _bias_table(rel_bias, n_heads):
    num_buckets = rel_bias.shape[0]
    ql = np.arange(BLOCK)[None, :]
    kl = np.arange(2 * BLOCK)[:, None]
    bucket = _t5_bucket(np.clip(ql + BLOCK - kl, 0, None), num_buckets).astype(np.int32)
    return pl.pallas_call(
        functools.partial(_bias_table_kernel, num_buckets=num_buckets),
        grid=(2, n_heads),
        in_specs=[pl.BlockSpec((2 * BLOCK, BLOCK), lambda f, h: (0, 0)),
                  pl.BlockSpec(memory_space=pltpu.SMEM)],
        out_specs=pl.BlockSpec((1, 1, 2 * BLOCK, BLOCK), lambda f, h: (f, h, 0, 0)),
        out_shape=jax.ShapeDtypeStruct((2, n_heads, 2 * BLOCK, BLOCK), F32),
        compiler_params=_cparams(("arbitrary", "arbitrary"), 32),
        name="swa_bias_table",
    )(jnp.asarray(bucket), rel_bias.astype(F32))


def _swa_kernel(q_ref, kp_ref, kc_ref, vp_ref, vc_ref, bias_ref, sink_ref, o_ref, *,
                n_kv, groups, dh, qk_scale):
    qt = q_ref[...].T
    k_band = jnp.concatenate([kp_ref[...], kc_ref[...]], axis=0)
    vt = jnp.concatenate([vp_ref[...], vc_ref[...]], axis=0).T
    outs = []
    for hk in range(n_kv):
        k = k_band[:, hk * dh:(hk + 1) * dh]
        q_grp = jnp.concatenate(
            [qt[(hk * groups + g) * dh:(hk * groups + g + 1) * dh, :] for g in range(groups)],
            axis=1)
        s = jnp.dot(k, q_grp, preferred_element_type=F32) * qk_scale
        probs, denoms = [], []
        for g in range(groups):
            hq = hk * groups + g
            sg = s[:, g * BLOCK:(g + 1) * BLOCK] + bias_ref[0, hq]
            sink = jnp.full((1, BLOCK), sink_ref[hq], F32) * LOG2E
            m = jnp.maximum(jnp.max(sg, axis=0, keepdims=True), sink)
            p = jnp.exp2(sg - m)
            denoms.append(jnp.sum(p, axis=0, keepdims=True) + jnp.exp2(sink - m))
            probs.append(p.astype(BF16))
        o = jnp.dot(vt[hk * dh:(hk + 1) * dh, :], jnp.concatenate(probs, axis=1),
                    preferred_element_type=F32)
        for g in range(groups):
            outs.append((o[:, g * BLOCK:(g + 1) * BLOCK] / denoms[g]).astype(o_ref.dtype))
    o_ref[...] = jnp.concatenate(outs, axis=0).T


def _swa_attention(qkv, bias, sinks, batch, seq, n_heads, n_kv, dh, q_col0):
    nb = seq // BLOCK
    qw = n_heads * dh
    kw = n_kv * dh
    assert q_col0 % qw == 0 and (q_col0 + qw) % kw == 0
    qb = q_col0 // qw
    kb = (q_col0 + qw) // kw
    vb = kb + 1
    prev = lambda b, i: b * nb + jnp.maximum(i - 1, 0)
    cur = lambda b, i: b * nb + i
    kern = functools.partial(_swa_kernel, n_kv=n_kv, groups=n_heads // n_kv, dh=dh,
                             qk_scale=dh ** -0.5 * LOG2E)
    return pl.pallas_call(
        kern,
        grid=(batch, nb),
        in_specs=[pl.BlockSpec((BLOCK, qw), lambda b, i: (cur(b, i), qb)),
                  pl.BlockSpec((BLOCK, kw), lambda b, i: (prev(b, i), kb)),
                  pl.BlockSpec((BLOCK, kw), lambda b, i: (cur(b, i), kb)),
                  pl.BlockSpec((BLOCK, kw), lambda b, i: (prev(b, i), vb)),
                  pl.BlockSpec((BLOCK, kw), lambda b, i: (cur(b, i), vb)),
                  pl.BlockSpec((1, n_heads, 2 * BLOCK, BLOCK),
                               lambda b, i: (jnp.minimum(i, 1), 0, 0, 0)),
                  pl.BlockSpec(memory_space=pltpu.SMEM)],
        out_specs=pl.BlockSpec((BLOCK, qw), lambda b, i: (cur(b, i), 0)),
        out_shape=jax.ShapeDtypeStruct((batch * seq, qw), BF16),
        compiler_params=_cparams(("parallel", "arbitrary"), 40),
        name="swa_attention",
    )(qkv, qkv, qkv, qkv, qkv, bias, sinks.astype(F32))


def kernel(x, norm1_g, w_in, b_forget, attn_sinks, rel_bias, w_branch_a, w_branch_b, w_out,
           norm2_g, w_ffn_gate, w_ffn_up, w_ffn_down, final_g):
    batch, seq, d = x.shape
    depth = w_in.shape[0]
    n_ha = b_forget.shape[1]
    n_hb = attn_sinks.shape[1]
    wa = w_branch_a.shape[1]
    wb = w_branch_b.shape[1]
    dh_a = wa // n_ha
    dh_b = wb // n_hb
    w_kvb = (w_in.shape[2] - 3 * wa - n_ha - wb - 2 * d) // 2
    n_kvb = w_kvb // dh_b
    c_f = 3 * wa
    c_qb = c_f + n_ha
    c_ga = c_qb + wb + 2 * w_kvb
    c_gb = c_ga + d

    xf = x.reshape(batch * seq, d)
    bias_tab = _bias_table(rel_bias, n_hb)

    for l in range(depth):
        w = w_in[l]
        w_qkv = jnp.concatenate([w[:, :c_f], w[:, c_qb:c_ga]], axis=1).astype(BF16)
        w_f = jnp.pad(w[:, c_f:c_qb], ((0, 0), (0, LANES - n_ha))).astype(BF16)
        b_f = jnp.pad(b_forget[l].astype(F32), (0, LANES - n_ha)).reshape(1, LANES)
        w_ga = w[:, c_ga:c_gb].astype(BF16)
        w_gb = w[:, c_gb:].astype(BF16)

        h1 = _rmsnorm(xf, norm1_g[l], BF16, "rmsnorm1")
        qkv = _matmul([h1], [w_qkv], [(0, 0)], [], _ep_identity, BF16, 1024, 512,
                      name="in_proj")
        eq, ek = _cumlog(h1, w_f, b_f, batch, seq, n_ha, dh_a)
        oa = _fox_attention(qkv, eq, ek, batch, seq, n_ha, dh_a)
        ob = _swa_attention(qkv, bias_tab, attn_sinks[l], batch, seq, n_hb, n_kvb, dh_b, c_f)
        mixed = _matmul([h1, oa, ob],
                        [w_ga, w_gb, w_branch_a[l].astype(BF16), w_branch_b[l].astype(BF16)],
                        [(0, 0), (0, 1), (1, 2), (2, 3)], [], _ep_gated_merge, BF16,
                        1024, 256, vmem_mib=56, name="gated_merge")
        xf = _matmul([mixed], [w_out[l].astype(BF16)], [(0, 0)], [xf], _ep_residual, F32,
                     1024, 512, name="out_proj")

        h2 = _rmsnorm(xf, norm2_g[l], BF16, "rmsnorm2")
        hidden = _matmul([h2], [w_ffn_gate[l].astype(BF16), w_ffn_up[l].astype(BF16)],
                         [(0, 0), (0, 1)], [], _ep_swiglu, BF16, 1024, 256, name="ffn_gate_up")
        xf = _matmul([hidden], [w_ffn_down[l].astype(BF16)], [(0, 0)], [xf], _ep_residual, F32,
                     256, 1024, n_outer=True, vmem_mib=56, name="ffn_down")

    out = _rmsnorm(xf, final_g, x.dtype, "rmsnorm_final")
    return out.reshape(batch, seq, d)
```

```python
import functools

import numpy as np
import jax
import jax.numpy as jnp
from jax import lax
from jax.experimental import pallas as pl
from jax.experimental.pallas import tpu as pltpu

F32 = jnp.float32
BF16 = jnp.bfloat16

EPS = 1e-6
WINDOW = 128
BLOCK = 128
MAX_DISTANCE = 128
LANES = 128
MIB = 1024 * 1024
LOG2E = 1.4426950408889634


def _cparams(semantics, vmem_mib):
    return pltpu.CompilerParams(dimension_semantics=semantics,
                                vmem_limit_bytes=vmem_mib * MIB)


def _rmsnorm_kernel(x_ref, g_ref, o_ref):
    x = x_ref[...].astype(F32)
    ms = jnp.mean(x * x, axis=-1, keepdims=True)
    o_ref[...] = (x * lax.rsqrt(ms + EPS) * g_ref[...]).astype(o_ref.dtype)


def _rmsnorm(x, g, out_dtype, name, tm=256):
    m, d = x.shape
    return pl.pallas_call(
        _rmsnorm_kernel,
        grid=(m // tm,),
        in_specs=[pl.BlockSpec((tm, d), lambda i: (i, 0)),
                  pl.BlockSpec((1, d), lambda i: (0, 0))],
        out_specs=pl.BlockSpec((tm, d), lambda i: (i, 0)),
        out_shape=jax.ShapeDtypeStruct((m, d), out_dtype),
        compiler_params=_cparams(("parallel",), 40),
        name=name,
    )(x, g.reshape(1, d).astype(F32))


def _mm_kernel(*refs, n_a, n_b, pairs, n_extra, epilogue):
    a_refs = refs[:n_a]
    b_refs = refs[n_a:n_a + n_b]
    e_refs = refs[n_a + n_b:n_a + n_b + n_extra]
    o_ref = refs[-1]
    accs = [jnp.dot(a_refs[ai][...], b_refs[bi][...], preferred_element_type=F32)
            for ai, bi in pairs]
    o_ref[...] = epilogue(accs, [e[...] for e in e_refs]).astype(o_ref.dtype)


def _matmul(a_list, b_list, pairs, extras, epilogue, out_dtype, bm, bn, *,
            n=None, n_outer=False, vmem_mib=48, name):
    m = a_list[0].shape[0]
    n = b_list[0].shape[1] if n is None else n
    assert m % bm == 0 and n % bn == 0
    if n_outer:
        grid = (n // bn, m // bm)
        mi = lambda j, i: i
        nj = lambda j, i: j
    else:
        grid = (m // bm, n // bn)
        mi = lambda i, j: i
        nj = lambda i, j: j
    in_specs = []
    for a in a_list:
        assert a.shape[0] == m
        in_specs.append(pl.BlockSpec((bm, a.shape[1]), lambda *g: (mi(*g), 0)))
    for b in b_list:
        assert b.shape[1] >= n
        if n_outer:
            spec = pl.BlockSpec((b.shape[0], bn), lambda *g: (0, nj(*g)),
                                pipeline_mode=pl.Buffered(1))
        else:
            spec = pl.BlockSpec((b.shape[0], bn), lambda *g: (0, nj(*g)))
        in_specs.append(spec)
    for e in extras:
        if e.shape[0] == 1:
            in_specs.append(pl.BlockSpec((1, bn), lambda *g: (0, nj(*g))))
        else:
            assert e.shape == (m, n)
            in_specs.append(pl.BlockSpec((bm, bn), lambda *g: (mi(*g), nj(*g))))
    kern = functools.partial(_mm_kernel, n_a=len(a_list), n_b=len(b_list), pairs=tuple(pairs),
                             n_extra=len(extras), epilogue=epilogue)
    return pl.pallas_call(
        kern,
        grid=grid,
        in_specs=in_specs,
        out_specs=pl.BlockSpec((bm, bn), lambda *g: (mi(*g), nj(*g))),
        out_shape=jax.ShapeDtypeStruct((m, n), out_dtype),
        compiler_params=_cparams(("parallel", "arbitrary"), vmem_mib),
        name=name,
    )(*a_list, *b_list, *extras)


def _ep_identity(accs, extras):
    return accs[0]


def _ep_residual(accs, extras):
    return extras[0] + accs[0]


def _ep_gated_merge(accs, extras):
    ga, gb, ya, yb = accs
    return jax.nn.sigmoid(ga) * ya + jax.nn.sigmoid(gb) * yb


def _ep_swiglu(accs, extras):
    g, u = accs
    return (g * jax.nn.sigmoid(g)) * u


N_SPLIT = 3


def _cumlog_kernel(h_ref, wf_ref, bf_ref, eq_ref, ek_ref, carry_ref, *, ts, n_heads, inv_scale):
    t = pl.program_id(1)

    @pl.when(t == 0)
    def _():
        carry_ref[...] = jnp.zeros_like(carry_ref)

    f = jnp.dot(h_ref[...], wf_ref[...], preferred_element_type=F32) + bf_ref[...]
    log_f = jnp.minimum(f, 0.0) - jnp.log(1.0 + jnp.exp(-jnp.abs(f)))
    row = lax.broadcasted_iota(jnp.int32, (ts, ts), 0)
    col = lax.broadcasted_iota(jnp.int32, (ts, ts), 1)
    tri = (col <= row).astype(F32)
    c = jnp.dot(tri, log_f, precision=lax.Precision.HIGHEST,
                preferred_element_type=F32) + carry_ref[...]
    carry_ref[...] = c[ts - 1:ts, :]
    lane = lax.broadcasted_iota(jnp.int32, (ts, LANES), 1)
    ones = jnp.where(lane < 2 * N_SPLIT, 1.0, 0.0)
    for h in range(n_heads):
        c3 = jnp.broadcast_to(c[:, h:h + 1] * inv_scale, (ts, LANES))
        hi = c3.astype(BF16).astype(F32)
        mid = (c3 - hi).astype(BF16).astype(F32)
        lo = (c3 - hi - mid).astype(BF16).astype(F32)
        eq = jnp.where(lane == 0, hi, jnp.where(lane == 1, mid, jnp.where(lane == 2, lo, ones)))
        ek = jnp.where(lane == N_SPLIT, -hi,
                       jnp.where(lane == N_SPLIT + 1, -mid,
                                 jnp.where(lane == N_SPLIT + 2, -lo, ones)))
        eq_ref[0, h] = eq.astype(BF16)
        ek_ref[0, h] = ek.astype(BF16)


def _cumlog(h1, w_f, b_f, batch, seq, n_heads, dh, ts=512):
    d = h1.shape[1]
    nt = seq // ts
    feat = pl.BlockSpec((1, n_heads, ts, LANES), lambda b, t: (b, 0, t, 0))
    return pl.pallas_call(
        functools.partial(_cumlog_kernel, ts=ts, n_heads=n_heads, inv_scale=dh ** 0.5),
        grid=(batch, nt),
        in_specs=[pl.BlockSpec((ts, d), lambda b, t: (b * nt + t, 0)),
                  pl.BlockSpec((d, LANES), lambda b, t: (0, 0)),
                  pl.BlockSpec((1, LANES), lambda b, t: (0, 0))],
        out_specs=[feat, feat],
        out_shape=[jax.ShapeDtypeStruct((batch, n_heads, seq, LANES), BF16)] * 2,
        scratch_shapes=[pltpu.VMEM((1, LANES), F32)],
        compiler_params=_cparams(("parallel", "arbitrary"), 40),
        name="fox_cumlog",
    )(h1, w_f, b_f)


FOX_BQ = 512
FOX_BK = 2 * FOX_BQ


def _fox_kernel(q_ref, k_ref, v_ref, eq_ref, ek_ref, o_ref, vt_ref, *, n_pairs, qk_scale):
    bq, bk = FOX_BQ, FOX_BK
    dh = q_ref.shape[1]

    for c in range(vt_ref.shape[0]):
        vt_ref[c] = v_ref[c * bq:(c + 1) * bq, :].T

    def scores(q, c0, q0, width, masked):
        k0 = c0 * bq
        k = jnp.concatenate([k_ref[k0:k0 + width, :], ek_ref[0, 0, k0:k0 + width, :]], axis=1)
        s = lax.dot_general(k, q, (((1,), (1,)), ((), ())),
                            preferred_element_type=F32)
        if masked:
            key = k0 + lax.broadcasted_iota(jnp.int32, (width, bq), 0)
            qry = q0 + lax.broadcasted_iota(jnp.int32, (width, bq), 1)
            s = jnp.where(key <= qry, s, -jnp.inf)
        return s

    def fold(s, carry, c0):
        m, l, acc = carry
        width = s.shape[0]
        m_new = jnp.maximum(m, jnp.max(s, axis=0, keepdims=True))
        alpha = jnp.exp2((m - m_new) * qk_scale)
        p = jnp.exp2((s - m_new) * qk_scale)
        l = alpha * l + jnp.sum(p, axis=0, keepdims=True)
        p = p.astype(BF16)
        pv = jnp.dot(vt_ref[c0], p[:bq, :], preferred_element_type=F32)
        for t in range(1, width // bq):
            pv = pv + jnp.dot(vt_ref[c0 + t], p[t * bq:(t + 1) * bq, :],
                              preferred_element_type=F32)
        return m_new, l, alpha * acc + pv

    tiles = []
    for mi in range(n_pairs):
        tiles += [(2 * mi, 2 * mi, bq, True), (2 * mi + 1, 2 * mi, bk, True)]
        for t in range(mi):
            tiles += [(2 * mi, 2 * t, bk, False), (2 * mi + 1, 2 * t, bk, False)]
    remaining = {}
    for blk, _, _, _ in tiles:
        remaining[blk] = remaining.get(blk, 0) + 1

    q_aug, state = {}, {}

    def issue(tile):
        blk, c0, width, masked = tile
        if blk not in q_aug:
            q0 = blk * bq
            q_aug[blk] = jnp.concatenate([q_ref[q0:q0 + bq, :], eq_ref[0, 0, q0:q0 + bq, :]], axis=1)
            state[blk] = (jnp.full((1, bq), -jnp.inf, F32), jnp.zeros((1, bq), F32),
                          jnp.zeros((dh, bq), F32))
        return scores(q_aug[blk], c0, blk * bq, width, masked)

    def retire(tile, s):
        blk, c0, _, _ = tile
        state[blk] = fold(s, state[blk], c0)
        remaining[blk] -= 1
        if remaining[blk] == 0:
            _, l, acc = state.pop(blk)
            q_aug.pop(blk)
            o_ref[blk * bq:(blk + 1) * bq, :] = (acc / l).T.astype(o_ref.dtype)

    pending = None
    for tile in tiles:
        s = issue(tile)
        if pending is not None:
            retire(*pending)
        pending = (tile, s)
    retire(*pending)


def _fox_attention(qkv, eq, ek, batch, seq, n_heads, dh):
    assert dh == LANES and seq % FOX_BK == 0
    nq = seq // FOX_BQ
    kern = functools.partial(_fox_kernel, n_pairs=nq // 2, qk_scale=dh ** -0.5 * LOG2E)
    feat = pl.BlockSpec((1, 1, seq, LANES), lambda b, h: (b, h, 0, 0))
    return pl.pallas_call(
        kern,
        grid=(batch, n_heads),
        in_specs=[pl.BlockSpec((seq, dh), lambda b, h: (b, h)),
                  pl.BlockSpec((seq, dh), lambda b, h: (b, n_heads + h)),
                  pl.BlockSpec((seq, dh), lambda b, h: (b, 2 * n_heads + h)),
                  feat, feat],
        out_specs=pl.BlockSpec((seq, dh), lambda b, h: (b, h)),
        out_shape=jax.ShapeDtypeStruct((batch * seq, n_heads * dh), BF16),
        scratch_shapes=[pltpu.VMEM((nq, dh, FOX_BQ), BF16)],
        compiler_params=_cparams(("parallel", "arbitrary"), 48),
        name="fox_attention",
    )(qkv, qkv, qkv, eq, ek)


def _t5_bucket(dist, num_buckets):
    max_exact = num_buckets // 2
    small = dist < max_exact
    large = max_exact + (np.log(np.maximum(dist, 1) / max_exact) / np.log(MAX_DISTANCE / max_exact)
                         * (num_buckets - max_exact)).astype(np.int64)
    large = np.minimum(large, num_buckets - 1)
    return np.where(small, dist, large)


def _bias_table_kernel(bucket_ref, rb_ref, o_ref, *, num_buckets):
    first = pl.program_id(0) == 0
    h = pl.program_id(1)
    bkt = bucket_ref[...]
    bias = jnp.zeros(bkt.shape, F32)
    for b in range(num_buckets):
        bias = jnp.where(bkt == b, rb_ref[b, h], bias)
    key = lax.broadcasted_iota(jnp.int32, bkt.shape, 0)
    qry = lax.broadcasted_iota(jnp.int32, bkt.shape, 1)
    dist = qry + BLOCK - key
    visible = (dist >= 0) & (dist < WINDOW) & ((key >= BLOCK) | jnp.logical_not(first))
    o_ref[0, 0] = jnp.where(visible, bias * LOG2E, -jnp.inf)


def _bias_table(rel_bias, n_heads):
    num_buckets = rel_bias.shape[0]
    ql = np.arange(BLOCK)[None, :]
    kl = np.arange(2 * BLOCK)[:, None]
    bucket = _t5_bucket(np.clip(ql + BLOCK - kl, 0, None), num_buckets).astype(np.int32)
    return pl.pallas_call(
        functools.partial(_bias_table_kernel, num_buckets=num_buckets),
        grid=(2, n_heads),
        in_specs=[pl.BlockSpec((2 * BLOCK, BLOCK), lambda f, h: (0, 0)),
                  pl.BlockSpec(memory_space=pltpu.SMEM)],
        out_specs=pl.BlockSpec((1, 1, 2 * BLOCK, BLOCK), lambda f, h: (f, h, 0, 0)),
        out_shape=jax.ShapeDtypeStruct((2, n_heads, 2 * BLOCK, BLOCK), F32),
        compiler_params=_cparams(("arbitrary", "arbitrary"), 32),
        name="swa_bias_table",
    )(jnp.asarray(bucket), rel_bias.astype(F32))


def _swa_kernel(q_ref, kp_ref, kc_ref, vp_ref, vc_ref, bias_ref, sink_ref, o_ref, *,
                n_kv, groups, dh, qk_scale):
    qt = q_ref[...].T
    k_band = jnp.concatenate([kp_ref[...], kc_ref[...]], axis=0)
    vt = jnp.concatenate([vp_ref[...], vc_ref[...]], axis=0).T
    outs = []
    for hk in range(n_kv):
        k = k_band[:, hk * dh:(hk + 1) * dh]
        q_grp = jnp.concatenate(
            [qt[(hk * groups + g) * dh:(hk * groups + g + 1) * dh, :] for g in range(groups)],
            axis=1)
        s = jnp.dot(k, q_grp, preferred_element_type=F32) * qk_scale
        probs, denoms = [], []
        for g in range(groups):
            hq = hk * groups + g
            sg = s[:, g * BLOCK:(g + 1) * BLOCK] + bias_ref[0, hq]
            sink = jnp.full((1, BLOCK), sink_ref[hq], F32) * LOG2E
            m = jnp.maximum(jnp.max(sg, axis=0, keepdims=True), sink)
            p = jnp.exp2(sg - m)
            denoms.append(jnp.sum(p, axis=0, keepdims=True) + jnp.exp2(sink - m))
            probs.append(p.astype(BF16))
        o = jnp.dot(vt[hk * dh:(hk + 1) * dh, :], jnp.concatenate(probs, axis=1),
                    preferred_element_type=F32)
        for g in range(groups):
            outs.append((o[:, g * BLOCK:(g + 1) * BLOCK] / denoms[g]).astype(o_ref.dtype))
    o_ref[...] = jnp.concatenate(outs, axis=0).T


def _swa_attention(qkv, bias, sinks, batch, seq, n_heads, n_kv, dh, q_col0):
    nb = seq // BLOCK
    qw = n_heads * dh
    kw = n_kv * dh
    assert q_col0 % qw == 0 and (q_col0 + qw) % kw == 0
    qb = q_col0 // qw
    kb = (q_col0 + qw) // kw
    vb = kb + 1
    prev = lambda b, i: b * nb + jnp.maximum(i - 1, 0)
    cur = lambda b, i: b * nb + i
    kern = functools.partial(_swa_kernel, n_kv=n_kv, groups=n_heads // n_kv, dh=dh,
                             qk_scale=dh ** -0.5 * LOG2E)
    return pl.pallas_call(
        kern,
        grid=(batch, nb),
        in_specs=[pl.BlockSpec((BLOCK, qw), lambda b, i: (cur(b, i), qb)),
                  pl.BlockSpec((BLOCK, kw), lambda b, i: (prev(b, i), kb)),
                  pl.BlockSpec((BLOCK, kw), lambda b, i: (cur(b, i), kb)),
                  pl.BlockSpec((BLOCK, kw), lambda b, i: (prev(b, i), vb)),
                  pl.BlockSpec((BLOCK, kw), lambda b, i: (cur(b, i), vb)),
                  pl.BlockSpec((1, n_heads, 2 * BLOCK, BLOCK),
                               lambda b, i: (jnp.minimum(i, 1), 0, 0, 0)),
                  pl.BlockSpec(memory_space=pltpu.SMEM)],
        out_specs=pl.BlockSpec((BLOCK, qw), lambda b, i: (cur(b, i), 0)),
        out_shape=jax.ShapeDtypeStruct((batch * seq, qw), BF16),
        compiler_params=_cparams(("parallel", "arbitrary"), 40),
        name="swa_attention",
    )(qkv, qkv, qkv, qkv, qkv, bias, sinks.astype(F32))


def kernel(x, norm1_g, w_in, b_forget, attn_sinks, rel_bias, w_branch_a, w_branch_b, w_out,
           norm2_g, w_ffn_gate, w_ffn_up, w_ffn_down, final_g):
    batch, seq, d = x.shape
    depth = w_in.shape[0]
    n_ha = b_forget.shape[1]
    n_hb = attn_sinks.shape[1]
    wa = w_branch_a.shape[1]
    wb = w_branch_b.shape[1]
    dh_a = wa // n_ha
    dh_b = wb // n_hb
    w_kvb = (w_in.shape[2] - 3 * wa - n_ha - wb - 2 * d) // 2
    n_kvb = w_kvb // dh_b
    c_f = 3 * wa
    c_qb = c_f + n_ha
    c_ga = c_qb + wb + 2 * w_kvb
    c_gb = c_ga + d

    xf = x.reshape(batch * seq, d)
    bias_tab = _bias_table(rel_bias, n_hb)

    for l in range(depth):
        w = w_in[l].astype(BF16)
        w_qkvb = w[:, c_qb:c_ga]
        w_f = jnp.pad(w[:, c_f:c_qb], ((0, 0), (0, LANES - n_ha)))
        b_f = jnp.pad(b_forget[l].astype(F32), (0, LANES - n_ha)).reshape(1, LANES)
        w_ga = w[:, c_ga:c_gb]
        w_gb = w[:, c_gb:]

        h1 = _rmsnorm(xf, norm1_g[l], BF16, "rmsnorm1")
        qkv_a = _matmul([h1], [w], [(0, 0)], [], _ep_identity, BF16, 2048, 512, n=c_f,
                        vmem_mib=56, name="in_proj_fox")
        qkv_b = _matmul([h1], [w_qkvb], [(0, 0)], [], _ep_identity, BF16, 2048, 512,
                        vmem_mib=56, name="in_proj_swa")
        eq, ek = _cumlog(h1, w_f, b_f, batch, seq, n_ha, dh_a)
        oa = _fox_attention(qkv_a, eq, ek, batch, seq, n_ha, dh_a)
        ob = _swa_attention(qkv_b, bias_tab, attn_sinks[l], batch, seq, n_hb, n_kvb, dh_b, 0)
        mixed = _matmul([h1, oa, ob],
                        [w_ga, w_gb, w_branch_a[l].astype(BF16), w_branch_b[l].astype(BF16)],
                        [(0, 0), (0, 1), (1, 2), (2, 3)], [], _ep_gated_merge, BF16,
                        1024, 256, vmem_mib=56, name="gated_merge")
        xf = _matmul([mixed], [w_out[l].astype(BF16)], [(0, 0)], [xf], _ep_residual, F32,
                     1024, 1024, vmem_mib=58, name="out_proj")

        h2 = _rmsnorm(xf, norm2_g[l], BF16, "rmsnorm2")
        hidden = _matmul([h2], [w_ffn_gate[l].astype(BF16), w_ffn_up[l].astype(BF16)],
                         [(0, 0), (0, 1)], [], _ep_swiglu, BF16, 2048, 256, vmem_mib=56,
                         name="ffn_gate_up")
        xf = _matmul([hidden], [w_ffn_down[l].astype(BF16)], [(0, 0)], [xf], _ep_residual, F32,
                     256, 1024, n_outer=True, vmem_mib=56, name="ffn_down")

    out = _rmsnorm(xf, final_g, x.dtype, "rmsnorm_final")
    return out.reshape(batch, seq, d)
```

```python
import functools

import numpy as np
import jax
import jax.numpy as jnp
from jax import lax
from jax.experimental import pallas as pl
from jax.experimental.pallas import tpu as pltpu

F32 = jnp.float32
BF16 = jnp.bfloat16

EPS = 1e-6
WINDOW = 128
BLOCK = 128
MAX_DISTANCE = 128
LANES = 128
BF16_ROWS = 16
MIB = 1024 * 1024
LOG2E = 1.4426950408889634


def _cparams(semantics, vmem_mib):
    return pltpu.CompilerParams(dimension_semantics=semantics,
                                vmem_limit_bytes=vmem_mib * MIB)


def _rmsnorm_kernel(x_ref, g_ref, o_ref):
    x = x_ref[...].astype(F32)
    ms = jnp.mean(x * x, axis=-1, keepdims=True)
    o_ref[...] = (x * lax.rsqrt(ms + EPS) * g_ref[...]).astype(o_ref.dtype)


def _rmsnorm(x, g, out_dtype, name, tm=256):
    m, d = x.shape
    return pl.pallas_call(
        _rmsnorm_kernel,
        grid=(m // tm,),
        in_specs=[pl.BlockSpec((tm, d), lambda i: (i, 0)),
                  pl.BlockSpec((1, d), lambda i: (0, 0))],
        out_specs=pl.BlockSpec((tm, d), lambda i: (i, 0)),
        out_shape=jax.ShapeDtypeStruct((m, d), out_dtype),
        compiler_params=_cparams(("parallel",), 40),
        name=name,
    )(x, g.reshape(1, d).astype(F32))


def _mm_kernel(*refs, n_a, n_b, pairs, n_extra, n_side, epilogue):
    n_in = n_a + n_b + n_extra + n_side
    a_refs = refs[:n_a]
    b_refs = refs[n_a:n_a + n_b]
    e_refs = refs[n_a + n_b:n_a + n_b + n_extra]
    side_in = refs[n_a + n_b + n_extra:n_in]
    o_ref = refs[n_in]
    side_out = refs[n_in + 1:n_in + 1 + n_side]
    accs = [jnp.dot(a_refs[ai][...], b_refs[bi][...], preferred_element_type=F32)
            for ai, bi in pairs]
    o_ref[...] = epilogue(accs, [e[...] for e in e_refs]).astype(o_ref.dtype)
    for src_ref, dst_ref in zip(side_in, side_out):
        dst_ref[...] = src_ref[...].astype(dst_ref.dtype)


def _side_specs(side, n_steps, step_index):
    specs = []
    for w in side:
        rows = w.shape[0] // n_steps
        assert rows * n_steps == w.shape[0] and rows % BF16_ROWS == 0
        specs.append(pl.BlockSpec((rows, w.shape[1]), lambda *g: (step_index(*g), 0)))
    return specs


def _matmul(a_list, b_list, pairs, extras, epilogue, out_dtype, bm, bn, *,
            n=None, n_outer=False, side=(), vmem_mib=48, name):
    m = a_list[0].shape[0]
    n = b_list[0].shape[1] if n is None else n
    assert m % bm == 0 and n % bn == 0
    if n_outer:
        grid = (n // bn, m // bm)
        mi = lambda j, i: i
        nj = lambda j, i: j
    else:
        grid = (m // bm, n // bn)
        mi = lambda i, j: i
        nj = lambda i, j: j
    in_specs = []
    for a in a_list:
        assert a.shape[0] == m
        in_specs.append(pl.BlockSpec((bm, a.shape[1]), lambda *g: (mi(*g), 0)))
    for b in b_list:
        assert b.shape[1] >= n
        if n_outer:
            spec = pl.BlockSpec((b.shape[0], bn), lambda *g: (0, nj(*g)),
                                pipeline_mode=pl.Buffered(1))
        else:
            spec = pl.BlockSpec((b.shape[0], bn), lambda *g: (0, nj(*g)))
        in_specs.append(spec)
    for e in extras:
        if e.shape[0] == 1:
            in_specs.append(pl.BlockSpec((1, bn), lambda *g: (0, nj(*g))))
        else:
            assert e.shape == (m, n)
            in_specs.append(pl.BlockSpec((bm, bn), lambda *g: (mi(*g), nj(*g))))
    side_specs = _side_specs(side, grid[0] * grid[1], lambda g0, g1: g0 * grid[1] + g1)
    kern = functools.partial(_mm_kernel, n_a=len(a_list), n_b=len(b_list), pairs=tuple(pairs),
                             n_extra=len(extras), n_side=len(side), epilogue=epilogue)
    outs = pl.pallas_call(
        kern,
        grid=grid,
        in_specs=in_specs + side_specs,
        out_specs=[pl.BlockSpec((bm, bn), lambda *g: (mi(*g), nj(*g)))] + side_specs,
        out_shape=[jax.ShapeDtypeStruct((m, n), out_dtype)]
        + [jax.ShapeDtypeStruct(w.shape, BF16) for w in side],
        compiler_params=_cparams(("parallel", "arbitrary"), vmem_mib),
        name=name,
    )(*a_list, *b_list, *extras, *side)
    return outs if side else outs[0]


def _ep_identity(accs, extras):
    return accs[0]


def _ep_residual(accs, extras):
    return extras[0] + accs[0]


def _ep_gated_merge(accs, extras):
    ga, gb, ya, yb = accs
    return jax.nn.sigmoid(ga) * ya + jax.nn.sigmoid(gb) * yb


def _ep_swiglu(accs, extras):
    g, u = accs
    return (g * jax.nn.sigmoid(g)) * u


N_SPLIT = 3


def _cumlog_kernel(h_ref, wf_ref, bf_ref, place_ref, eq_ref, ek_ref, carry_ref, *,
                   ts, n_heads, inv_scale):
    t = pl.program_id(1)

    @pl.when(t == 0)
    def _():
        carry_ref[...] = jnp.zeros_like(carry_ref)

    f = jnp.dot(h_ref[...], wf_ref[...], preferred_element_type=F32) + bf_ref[...]
    log_f = jnp.minimum(f, 0.0) - jnp.log(1.0 + jnp.exp(-jnp.abs(f)))
    row = lax.broadcasted_iota(jnp.int32, (ts, ts), 0)
    col = lax.broadcasted_iota(jnp.int32, (ts, ts), 1)
    tri = (col <= row).astype(F32)
    c = jnp.dot(tri, log_f, precision=lax.Precision.HIGHEST,
                preferred_element_type=F32) + carry_ref[...]
    carry_ref[...] = c[ts - 1:ts, :]
    c3 = c * inv_scale
    hi = c3.astype(BF16).astype(F32)
    mid = (c3 - hi).astype(BF16).astype(F32)
    lo = (c3 - hi - mid).astype(BF16).astype(F32)
    lane = lax.broadcasted_iota(jnp.int32, (ts, LANES), 1)
    packed = jnp.where(
        lane < n_heads, hi,
        jnp.where(lane < 2 * n_heads, pltpu.roll(mid, n_heads, 1),
                  jnp.where(lane < N_SPLIT * n_heads, pltpu.roll(lo, 2 * n_heads, 1),
                            jnp.where(lane == N_SPLIT * n_heads, 1.0, 0.0))))
    feats = jnp.dot(packed.astype(BF16), place_ref[...], preferred_element_type=F32)
    for h in range(n_heads):
        eq_ref[0, h] = feats[:, h * LANES:(h + 1) * LANES].astype(BF16)
        ek_ref[0, h] = feats[:, (n_heads + h) * LANES:(n_heads + h + 1) * LANES].astype(BF16)


def _placement(n_heads):
    p = np.zeros((LANES, 2 * n_heads * LANES), np.float32)
    one = N_SPLIT * n_heads
    for h in range(n_heads):
        for j in range(N_SPLIT):
            p[j * n_heads + h, h * LANES + j] = 1.0
            p[j * n_heads + h, (n_heads + h) * LANES + N_SPLIT + j] = -1.0
            p[one, h * LANES + N_SPLIT + j] = 1.0
            p[one, (n_heads + h) * LANES + j] = 1.0
    return jnp.asarray(p, BF16)


def _cumlog(h1, w_f, b_f, batch, seq, n_heads, dh, ts=512):
    assert (N_SPLIT * n_heads) < LANES
    d = h1.shape[1]
    nt = seq // ts
    feat = pl.BlockSpec((1, n_heads, ts, LANES), lambda b, t: (b, 0, t, 0))
    return pl.pallas_call(
        functools.partial(_cumlog_kernel, ts=ts, n_heads=n_heads, inv_scale=dh ** 0.5),
        grid=(batch, nt),
        in_specs=[pl.BlockSpec((ts, d), lambda b, t: (b * nt + t, 0)),
                  pl.BlockSpec((d, LANES), lambda b, t: (0, 0)),
                  pl.BlockSpec((1, LANES), lambda b, t: (0, 0)),
                  pl.BlockSpec((LANES, 2 * n_heads * LANES), lambda b, t: (0, 0))],
        out_specs=[feat, feat],
        out_shape=[jax.ShapeDtypeStruct((batch, n_heads, seq, LANES), BF16)] * 2,
        scratch_shapes=[pltpu.VMEM((1, LANES), F32)],
        compiler_params=_cparams(("parallel", "arbitrary"), 40),
        name="fox_cumlog",
    )(h1, w_f, b_f, _placement(n_heads))


FOX_BQ = 512
FOX_BK = 2 * FOX_BQ


def _fox_kernel(q_ref, k_ref, v_ref, eq_ref, ek_ref, *rest, n_side, n_pairs, qk_scale):
    bq, bk = FOX_BQ, FOX_BK
    dh = q_ref.shape[1]
    side_in, o_ref = rest[:n_side], rest[n_side]
    side_out, vt_ref = rest[n_side + 1:2 * n_side + 1], rest[2 * n_side + 1]
    for src_ref, dst_ref in zip(side_in, side_out):
        dst_ref[...] = src_ref[...].astype(dst_ref.dtype)

    for c in range(vt_ref.shape[0]):
        vt_ref[c] = v_ref[c * bq:(c + 1) * bq, :].T

    def scores(q, c0, q0, width, masked):
        k0 = c0 * bq
        k = jnp.concatenate([k_ref[k0:k0 + width, :], ek_ref[0, 0, k0:k0 + width, :]], axis=1)
        s = lax.dot_general(k, q, (((1,), (1,)), ((), ())),
                            preferred_element_type=F32)
        if masked:
            key = k0 + lax.broadcasted_iota(jnp.int32, (width, bq), 0)
            qry = q0 + lax.broadcasted_iota(jnp.int32, (width, bq), 1)
            s = jnp.where(key <= qry, s, -jnp.inf)
        return s

    def fold(s, carry, c0):
        m, l, acc = carry
        width = s.shape[0]
        m_new = jnp.maximum(m, jnp.max(s, axis=0, keepdims=True))
        alpha = jnp.exp2((m - m_new) * qk_scale)
        p = jnp.exp2((s - m_new) * qk_scale)
        l = alpha * l + jnp.sum(p, axis=0, keepdims=True)
        p = p.astype(BF16)
        pv = jnp.dot(vt_ref[c0], p[:bq, :], preferred_element_type=F32)
        for t in range(1, width // bq):
            pv = pv + jnp.dot(vt_ref[c0 + t], p[t * bq:(t + 1) * bq, :],
                              preferred_element_type=F32)
        return m_new, l, alpha * acc + pv

    tiles = []
    for mi in range(n_pairs):
        tiles += [(2 * mi, 2 * mi, bq, True), (2 * mi + 1, 2 * mi, bk, True)]
        for t in range(mi):
            tiles += [(2 * mi, 2 * t, bk, False), (2 * mi + 1, 2 * t, bk, False)]
    remaining = {}
    for blk, _, _, _ in tiles:
        remaining[blk] = remaining.get(blk, 0) + 1

    q_aug, state = {}, {}

    def issue(tile):
        blk, c0, width, masked = tile
        if blk not in q_aug:
            q0 = blk * bq
            q_aug[blk] = jnp.concatenate([q_ref[q0:q0 + bq, :], eq_ref[0, 0, q0:q0 + bq, :]], axis=1)
            state[blk] = (jnp.full((1, bq), -jnp.inf, F32), jnp.zeros((1, bq), F32),
                          jnp.zeros((dh, bq), F32))
        return scores(q_aug[blk], c0, blk * bq, width, masked)

    def retire(tile, s):
        blk, c0, _, _ = tile
        state[blk] = fold(s, state[blk], c0)
        remaining[blk] -= 1
        if remaining[blk] == 0:
            _, l, acc = state.pop(blk)
            q_aug.pop(blk)
            o_ref[blk * bq:(blk + 1) * bq, :] = (acc / l).T.astype(o_ref.dtype)

    pending = None
    for tile in tiles:
        s = issue(tile)
        if pending is not None:
            retire(*pending)
        pending = (tile, s)
    retire(*pending)


def _fox_attention(qkv, eq, ek, batch, seq, n_heads, dh, side=()):
    assert dh == LANES and seq % FOX_BK == 0
    nq = seq // FOX_BQ
    kern = functools.partial(_fox_kernel, n_side=len(side), n_pairs=nq // 2,
                             qk_scale=dh ** -0.5 * LOG2E)
    feat = pl.BlockSpec((1, 1, seq, LANES), lambda b, h: (b, h, 0, 0))
    side_specs = _side_specs(side, batch * n_heads, lambda b, h: b * n_heads + h)
    outs = pl.pallas_call(
        kern,
        grid=(batch, n_heads),
        in_specs=[pl.BlockSpec((seq, dh), lambda b, h: (b, h)),
                  pl.BlockSpec((seq, dh), lambda b, h: (b, n_heads + h)),
                  pl.BlockSpec((seq, dh), lambda b, h: (b, 2 * n_heads + h)),
                  feat, feat] + side_specs,
        out_specs=[pl.BlockSpec((seq, dh), lambda b, h: (b, h))] + side_specs,
        out_shape=[jax.ShapeDtypeStruct((batch * seq, n_heads * dh), BF16)]
        + [jax.ShapeDtypeStruct(w.shape, BF16) for w in side],
        scratch_shapes=[pltpu.VMEM((nq, dh, FOX_BQ), BF16)],
        compiler_params=_cparams(("parallel", "arbitrary"), 48),
        name="fox_attention",
    )(qkv, qkv, qkv, eq, ek, *side)
    return outs if side else outs[0]


def _t5_bucket(dist, num_buckets):
    max_exact = num_buckets // 2
    small = dist < max_exact
    large = max_exact + (np.log(np.maximum(dist, 1) / max_exact) / np.log(MAX_DISTANCE / max_exact)
                         * (num_buckets - max_exact)).astype(np.int64)
    large = np.minimum(large, num_buckets - 1)
    return np.where(small, dist, large)


def _bias_table_kernel(bucket_ref, rb_ref, o_ref, *, num_buckets):
    first = pl.program_id(0) == 0
    h = pl.program_id(1)
    bkt = bucket_ref[...]
    bias = jnp.zeros(bkt.shape, F32)
    for b in range(num_buckets):
        bias = jnp.where(bkt == b, rb_ref[b, h], bias)
    key = lax.broadcasted_iota(jnp.int32, bkt.shape, 0)
    qry = lax.broadcasted_iota(jnp.int32, bkt.shape, 1)
    dist = qry + BLOCK - key
    visible = (dist >= 0) & (dist < WINDOW) & ((key >= BLOCK) | jnp.logical_not(first))
    o_ref[0, 0] = jnp.where(visible, bias * LOG2E, -jnp.inf)


def _bias_table(rel_bias, n_heads):
    num_buckets = rel_bias.shape[0]
    ql = np.arange(BLOCK)[None, :]
    kl = np.arange(2 * BLOCK)[:, None]
    bucket = _t5_bucket(np.clip(ql + BLOCK - kl, 0, None), num_buckets).astype(np.int32)
    return pl.pallas_call(
        functools.partial(_bias_table_kernel, num_buckets=num_buckets),
        grid=(2, n_heads),
        in_specs=[pl.BlockSpec((2 * BLOCK, BLOCK), lambda f, h: (0, 0)),
                  pl.BlockSpec(memory_space=pltpu.SMEM)],
        out_specs=pl.BlockSpec((1, 1, 2 * BLOCK, BLOCK), lambda f, h: (f, h, 0, 0)),
        out_shape=jax.ShapeDtypeStruct((2, n_heads, 2 * BLOCK, BLOCK), F32),
        compiler_params=_cparams(("arbitrary", "arbitrary"), 32),
        name="swa_bias_table",
    )(jnp.asarray(bucket), rel_bias.astype(F32))


def _swa_kernel(q_ref, kp_ref, kc_ref, vp_ref, vc_ref, bias_ref, sink_ref, o_ref, *,
                n_kv, groups, dh, qk_scale):
    qt = q_ref[...].T
    k_band = jnp.concatenate([kp_ref[...], kc_ref[...]], axis=0)
    vt = jnp.concatenate([vp_ref[...], vc_ref[...]], axis=0).T
    outs = []
    for hk in range(n_kv):
        k = k_band[:, hk * dh:(hk + 1) * dh]
        q_grp = jnp.concatenate(
            [qt[(hk * groups + g) * dh:(hk * groups + g + 1) * dh, :] for g in range(groups)],
            axis=1)
        s = jnp.dot(k, q_grp, preferred_element_type=F32) * qk_scale
        probs, denoms = [], []
        for g in range(groups):
            hq = hk * groups + g
            sg = s[:, g * BLOCK:(g + 1) * BLOCK] + bias_ref[0, hq]
            sink = jnp.full((1, BLOCK), sink_ref[hq], F32) * LOG2E
            m = jnp.maximum(jnp.max(sg, axis=0, keepdims=True), sink)
            p = jnp.exp2(sg - m)
            denoms.append(jnp.sum(p, axis=0, keepdims=True) + jnp.exp2(sink - m))
            probs.append(p.astype(BF16))
        o = jnp.dot(vt[hk * dh:(hk + 1) * dh, :], jnp.concatenate(probs, axis=1),
                    preferred_element_type=F32)
        for g in range(groups):
            outs.append((o[:, g * BLOCK:(g + 1) * BLOCK] / denoms[g]).astype(o_ref.dtype))
    o_ref[...] = jnp.concatenate(outs, axis=0).T


def _swa_attention(qkv, bias, sinks, batch, seq, n_heads, n_kv, dh, q_col0):
    nb = seq // BLOCK
    qw = n_heads * dh
    kw = n_kv * dh
    assert q_col0 % qw == 0 and (q_col0 + qw) % kw == 0
    qb = q_col0 // qw
    kb = (q_col0 + qw) // kw
    vb = kb + 1
    prev = lambda b, i: b * nb + jnp.maximum(i - 1, 0)
    cur = lambda b, i: b * nb + i
    kern = functools.partial(_swa_kernel, n_kv=n_kv, groups=n_heads // n_kv, dh=dh,
                             qk_scale=dh ** -0.5 * LOG2E)
    return pl.pallas_call(
        kern,
        grid=(batch, nb),
        in_specs=[pl.BlockSpec((BLOCK, qw), lambda b, i: (cur(b, i), qb)),
                  pl.BlockSpec((BLOCK, kw), lambda b, i: (prev(b, i), kb)),
                  pl.BlockSpec((BLOCK, kw), lambda b, i: (cur(b, i), kb)),
                  pl.BlockSpec((BLOCK, kw), lambda b, i: (prev(b, i), vb)),
                  pl.BlockSpec((BLOCK, kw), lambda b, i: (cur(b, i), vb)),
                  pl.BlockSpec((1, n_heads, 2 * BLOCK, BLOCK),
                               lambda b, i: (jnp.minimum(i, 1), 0, 0, 0)),
                  pl.BlockSpec(memory_space=pltpu.SMEM)],
        out_specs=pl.BlockSpec((BLOCK, qw), lambda b, i: (cur(b, i), 0)),
        out_shape=jax.ShapeDtypeStruct((batch * seq, qw), BF16),
        compiler_params=_cparams(("parallel", "arbitrary"), 40),
        name="swa_attention",
    )(qkv, qkv, qkv, qkv, qkv, bias, sinks.astype(F32))


def kernel(x, norm1_g, w_in, b_forget, attn_sinks, rel_bias, w_branch_a, w_branch_b, w_out,
           norm2_g, w_ffn_gate, w_ffn_up, w_ffn_down, final_g):
    batch, seq, d = x.shape
    depth = w_in.shape[0]
    n_ha = b_forget.shape[1]
    n_hb = attn_sinks.shape[1]
    wa = w_branch_a.shape[1]
    wb = w_branch_b.shape[1]
    dh_a = wa // n_ha
    dh_b = wb // n_hb
    w_kvb = (w_in.shape[2] - 3 * wa - n_ha - wb - 2 * d) // 2
    n_kvb = w_kvb // dh_b
    c_f = 3 * wa
    c_qb = c_f + n_ha
    c_ga = c_qb + wb + 2 * w_kvb
    c_gb = c_ga + d

    xf = x.reshape(batch * seq, d)
    bias_tab = _bias_table(rel_bias, n_hb)

    for l in range(depth):
        w = w_in[l].astype(BF16)
        w_qkvb = w[:, c_qb:c_ga]
        w_f = jnp.pad(w[:, c_f:c_qb], ((0, 0), (0, LANES - n_ha)))
        b_f = jnp.pad(b_forget[l].astype(F32), (0, LANES - n_ha)).reshape(1, LANES)
        w_ga = w[:, c_ga:c_gb]
        w_gb = w[:, c_gb:]

        h1 = _rmsnorm(xf, norm1_g[l], BF16, "rmsnorm1")
        qkv_a = _matmul([h1], [w], [(0, 0)], [], _ep_identity, BF16, 2048, 512, n=c_f,
                        vmem_mib=56, name="in_proj_fox")
        qkv_b = _matmul([h1], [w_qkvb], [(0, 0)], [], _ep_identity, BF16, 2048, 512,
                        vmem_mib=56, name="in_proj_swa")
        eq, ek = _cumlog(h1, w_f, b_f, batch, seq, n_ha, dh_a)
        oa, w_out_b, w_bra, w_brb = _fox_attention(
            qkv_a, eq, ek, batch, seq, n_ha, dh_a, side=(w_out[l], w_branch_a[l], w_branch_b[l]))
        ob = _swa_attention(qkv_b, bias_tab, attn_sinks[l], batch, seq, n_hb, n_kvb, dh_b, 0)
        mixed, w_gate_b, w_up_b = _matmul(
            [h1, oa, ob], [w_ga, w_gb, w_bra, w_brb], [(0, 0), (0, 1), (1, 2), (2, 3)], [],
            _ep_gated_merge, BF16, 1024, 256, side=(w_ffn_gate[l], w_ffn_up[l]),
            vmem_mib=60, name="gated_merge")
        xf = _matmul([mixed], [w_out_b], [(0, 0)], [xf], _ep_residual, F32,
                     1024, 1024, vmem_mib=58, name="out_proj")

        h2 = _rmsnorm(xf, norm2_g[l], BF16, "rmsnorm2")
        hidden, w_down_b = _matmul([h2], [w_gate_b, w_up_b], [(0, 0), (0, 1)], [], _ep_swiglu,
                                   BF16, 2048, 256, side=(w_ffn_down[l],), vmem_mib=56,
                                   name="ffn_gate_up")
        xf = _matmul([hidden], [w_down_b], [(0, 0)], [xf], _ep_residual, F32,
                     256, 1024, n_outer=True, vmem_mib=56, name="ffn_down")

    out = _rmsnorm(xf, final_g, x.dtype, "rmsnorm_final")
    return out.reshape(batch, seq, d)
```

```python
import functools

import numpy as np
import jax
import jax.numpy as jnp
from jax import lax
from jax.experimental import pallas as pl
from jax.experimental.pallas import tpu as pltpu

F32 = jnp.float32
BF16 = jnp.bfloat16

EPS = 1e-6
WINDOW = 128
BLOCK = 128
MAX_DISTANCE = 128
LANES = 128
BF16_ROWS = 16
MM_ROW_CHUNK = 512
MIB = 1024 * 1024
LOG2E = 1.4426950408889634


def _cparams(semantics, vmem_mib):
    return pltpu.CompilerParams(dimension_semantics=semantics,
                                vmem_limit_bytes=vmem_mib * MIB)


def _rmsnorm_kernel(x_ref, g_ref, o_ref):
    x = x_ref[...].astype(F32)
    ms = jnp.mean(x * x, axis=-1, keepdims=True)
    o_ref[...] = (x * lax.rsqrt(ms + EPS) * g_ref[...]).astype(o_ref.dtype)


def _rmsnorm(x, g, out_dtype, name, tm=256):
    m, d = x.shape
    return pl.pallas_call(
        _rmsnorm_kernel,
        grid=(m // tm,),
        in_specs=[pl.BlockSpec((tm, d), lambda i: (i, 0)),
                  pl.BlockSpec((1, d), lambda i: (0, 0))],
        out_specs=pl.BlockSpec((tm, d), lambda i: (i, 0)),
        out_shape=jax.ShapeDtypeStruct((m, d), out_dtype),
        compiler_params=_cparams(("parallel",), 40),
        name=name,
    )(x, g.reshape(1, d).astype(F32))


def _mm_kernel(*refs, n_a, n_b, pairs, n_extra, n_side, n_out, acc_outs, row_chunk, epilogue):
    n_in = n_a + n_b + n_extra + n_side
    a_refs = refs[:n_a]
    b_refs = refs[n_a:n_a + n_b]
    e_refs = refs[n_a + n_b:n_a + n_b + n_extra]
    side_in = refs[n_a + n_b + n_extra:n_in]
    o_refs = refs[n_in:n_in + n_out]
    side_out = refs[n_in + n_out:n_in + n_out + n_side]
    bm = o_refs[0].shape[-2]

    def extra_block(e_ref, rows):
        return e_ref[...] if e_ref.shape[0] == 1 else e_ref[rows, :]

    def finish(rows, accs):
        results = epilogue(accs, [extra_block(e, rows) for e in e_refs])
        for k, (o_ref, r) in enumerate(zip(o_refs, results if n_out > 1 else (results,))):
            if k in acc_outs:
                first = pl.program_id(1) == 0

                @pl.when(first)
                def _(o_ref=o_ref, r=r):
                    o_ref[rows, :] = r.astype(o_ref.dtype)

                @pl.when(jnp.logical_not(first))
                def _(o_ref=o_ref, r=r):
                    o_ref[rows, :] += r.astype(o_ref.dtype)
            elif len(o_ref.shape) == 3:
                dh = o_ref.shape[2]
                for t in range(o_ref.shape[0]):
                    o_ref[t, rows, :] = r[:, t * dh:(t + 1) * dh].astype(o_ref.dtype)
            else:
                o_ref[rows, :] = r.astype(o_ref.dtype)

    pending = None
    for r in range(bm // row_chunk):
        rows = pl.ds(r * row_chunk, row_chunk)
        accs = [jnp.dot(a_refs[ai][rows, :], b_refs[bi][...], preferred_element_type=F32)
                for ai, bi in pairs]
        if pending is not None:
            finish(*pending)
        pending = (rows, accs)
    finish(*pending)
    for src_ref, dst_ref in zip(side_in, side_out):
        dst_ref[...] = src_ref[...].astype(dst_ref.dtype)


def _side_specs(side, n_steps, step_index):
    specs = []
    for w in side:
        rows = w.shape[0] // n_steps
        assert rows * n_steps == w.shape[0] and rows % BF16_ROWS == 0
        specs.append(pl.BlockSpec((rows, w.shape[1]), lambda *g: (step_index(*g), 0)))
    return specs


def _matmul(a_list, b_list, pairs, extras, epilogue, out_dtype, bm, bn, *,
            n=None, n_outer=False, side=(), more_outs=(), row_chunk=None, head_dim=None,
            vmem_mib=48, name):
    b_list = [b if isinstance(b, tuple) else (b, 0) for b in b_list]
    row_chunk = bm if row_chunk is None else row_chunk
    assert bm % row_chunk == 0
    m = a_list[0].shape[0]
    n = b_list[0][0].shape[1] - b_list[0][1] if n is None else n
    assert m % bm == 0 and n % bn == 0
    if n_outer:
        grid = (n // bn, m // bm)
        mi = lambda j, i: i
        nj = lambda j, i: j
    else:
        grid = (m // bm, n // bn)
        mi = lambda i, j: i
        nj = lambda i, j: j
    in_specs = []
    for a in a_list:
        assert a.shape[0] == m
        in_specs.append(pl.BlockSpec((bm, a.shape[1]), lambda *g: (mi(*g), 0)))
    for b, col0 in b_list:
        assert col0 % bn == 0 and b.shape[1] >= col0 + n
        index = lambda *g, off=col0 // bn: (0, off + nj(*g))
        if n_outer:
            in_specs.append(pl.BlockSpec((b.shape[0], bn), index, pipeline_mode=pl.Buffered(1)))
        else:
            in_specs.append(pl.BlockSpec((b.shape[0], bn), index))
    tile_spec = pl.BlockSpec((bm, bn), lambda *g: (mi(*g), nj(*g)))
    for e in extras:
        if e.shape[0] == 1:
            in_specs.append(pl.BlockSpec((1, bn), lambda *g: (0, nj(*g))))
        elif e.shape == (m, n):
            in_specs.append(tile_spec)
        else:
            assert e.shape[0] == m
            in_specs.append(pl.BlockSpec((bm, e.shape[1]), lambda *g: (mi(*g), 0)))
    out_specs, out_shapes, acc_outs = [tile_spec], [jax.ShapeDtypeStruct((m, n), out_dtype)], []
    if head_dim is not None:
        assert bn % head_dim == 0 and not more_outs
        out_specs = [pl.BlockSpec((bn // head_dim, bm, head_dim), lambda *g: (nj(*g), mi(*g), 0))]
        out_shapes = [jax.ShapeDtypeStruct((n // head_dim, m, head_dim), out_dtype)]
    for dtype, kind in more_outs:
        if kind == "tile":
            out_specs.append(tile_spec)
            out_shapes.append(jax.ShapeDtypeStruct((m, n), dtype))
        else:
            assert kind == "rowacc" and not n_outer
            acc_outs.append(len(out_specs))
            out_specs.append(pl.BlockSpec((bm, LANES), lambda *g: (mi(*g), 0)))
            out_shapes.append(jax.ShapeDtypeStruct((m, LANES), dtype))
    side_specs = _side_specs(side, grid[0] * grid[1], lambda g0, g1: g0 * grid[1] + g1)
    kern = functools.partial(_mm_kernel, n_a=len(a_list), n_b=len(b_list), pairs=tuple(pairs),
                             n_extra=len(extras), n_side=len(side), n_out=len(out_specs),
                             acc_outs=tuple(acc_outs), row_chunk=row_chunk, epilogue=epilogue)
    outs = pl.pallas_call(
        kern,
        grid=grid,
        in_specs=in_specs + side_specs,
        out_specs=out_specs + side_specs,
        out_shape=out_shapes + [jax.ShapeDtypeStruct(w.shape, BF16) for w in side],
        compiler_params=_cparams(("parallel", "arbitrary"), vmem_mib),
        name=name,
    )(*a_list, *[b for b, _ in b_list], *extras, *side)
    return outs if len(outs) > 1 else outs[0]


def _ep_identity(accs, extras):
    return accs[0]


def _ep_residual(accs, extras):
    return extras[0] + accs[0]


def _ep_gated_merge(accs, extras):
    ga, gb, ya, yb = accs
    return jax.nn.sigmoid(ga) * ya + jax.nn.sigmoid(gb) * yb


def _ep_swiglu(accs, extras):
    g, u = accs
    return (g * jax.nn.sigmoid(g)) * u


N_SPLIT = 3


def _cumlog_kernel(h_ref, wf_ref, bf_ref, place_ref, eq_ref, ek_ref, carry_ref, *,
                   ts, n_heads, inv_scale):
    t = pl.program_id(1)

    @pl.when(t == 0)
    def _():
        carry_ref[...] = jnp.zeros_like(carry_ref)

    f = jnp.dot(h_ref[...], wf_ref[...], preferred_element_type=F32) + bf_ref[...]
    log_f = jnp.minimum(f, 0.0) - jnp.log(1.0 + jnp.exp(-jnp.abs(f)))
    row = lax.broadcasted_iota(jnp.int32, (ts, ts), 0)
    col = lax.broadcasted_iota(jnp.int32, (ts, ts), 1)
    tri = (col <= row).astype(F32)
    c = jnp.dot(tri, log_f, precision=lax.Precision.HIGHEST,
                preferred_element_type=F32) + carry_ref[...]
    carry_ref[...] = c[ts - 1:ts, :]
    c3 = c * inv_scale
    hi = c3.astype(BF16).astype(F32)
    mid = (c3 - hi).astype(BF16).astype(F32)
    lo = (c3 - hi - mid).astype(BF16).astype(F32)
    lane = lax.broadcasted_iota(jnp.int32, (ts, LANES), 1)
    packed = jnp.where(
        lane < n_heads, hi,
        jnp.where(lane < 2 * n_heads, pltpu.roll(mid, n_heads, 1),
                  jnp.where(lane < N_SPLIT * n_heads, pltpu.roll(lo, 2 * n_heads, 1),
                            jnp.where(lane == N_SPLIT * n_heads, 1.0, 0.0))))
    feats = jnp.dot(packed.astype(BF16), place_ref[...], preferred_element_type=F32)
    for h in range(n_heads):
        eq_ref[0, h] = feats[:, h * LANES:(h + 1) * LANES].astype(BF16)
        ek_ref[0, h] = feats[:, (n_heads + h) * LANES:(n_heads + h + 1) * LANES].astype(BF16)


def _placement(n_heads):
    p = np.zeros((LANES, 2 * n_heads * LANES), np.float32)
    one = N_SPLIT * n_heads
    for h in range(n_heads):
        for j in range(N_SPLIT):
            p[j * n_heads + h, h * LANES + j] = 1.0
            p[j * n_heads + h, (n_heads + h) * LANES + N_SPLIT + j] = -1.0
            p[one, h * LANES + N_SPLIT + j] = 1.0
            p[one, (n_heads + h) * LANES + j] = 1.0
    return jnp.asarray(p, BF16)


def _cumlog(h1, w_f, b_f, batch, seq, n_heads, dh, ts=512):
    assert (N_SPLIT * n_heads) < LANES
    d = h1.shape[1]
    nt = seq // ts
    feat = pl.BlockSpec((1, n_heads, ts, LANES), lambda b, t: (b, 0, t, 0))
    return pl.pallas_call(
        functools.partial(_cumlog_kernel, ts=ts, n_heads=n_heads, inv_scale=dh ** 0.5),
        grid=(batch, nt),
        in_specs=[pl.BlockSpec((ts, d), lambda b, t: (b * nt + t, 0)),
                  pl.BlockSpec((d, LANES), lambda b, t: (0, 0)),
                  pl.BlockSpec((1, LANES), lambda b, t: (0, 0)),
                  pl.BlockSpec((LANES, 2 * n_heads * LANES), lambda b, t: (0, 0))],
        out_specs=[feat, feat],
        out_shape=[jax.ShapeDtypeStruct((batch, n_heads, seq, LANES), BF16)] * 2,
        scratch_shapes=[pltpu.VMEM((1, LANES), F32)],
        compiler_params=_cparams(("parallel", "arbitrary"), 40),
        name="fox_cumlog",
    )(h1, w_f, b_f, _placement(n_heads))


FOX_BQ = 512
FOX_BK = 2 * FOX_BQ


def _fox_kernel(q_ref, k_ref, v_ref, eq_ref, ek_ref, *rest, n_side, n_pairs, qk_scale):
    bq, bk = FOX_BQ, FOX_BK
    dh = q_ref.shape[2]
    side_in, o_ref = rest[:n_side], rest[n_side]
    side_out, vt_ref = rest[n_side + 1:2 * n_side + 1], rest[2 * n_side + 1]
    for src_ref, dst_ref in zip(side_in, side_out):
        dst_ref[...] = src_ref[...].astype(dst_ref.dtype)

    for c in range(vt_ref.shape[0]):
        vt_ref[c] = v_ref[0, c * bq:(c + 1) * bq, :].T

    def scores(q, c0, q0, width, masked):
        k0 = c0 * bq
        k = jnp.concatenate([k_ref[0, k0:k0 + width, :], ek_ref[0, 0, k0:k0 + width, :]], axis=1)
        s = lax.dot_general(k, q, (((1,), (1,)), ((), ())),
                            preferred_element_type=F32)
        if masked:
            key = k0 + lax.broadcasted_iota(jnp.int32, (width, bq), 0)
            qry = q0 + lax.broadcasted_iota(jnp.int32, (width, bq), 1)
            s = jnp.where(key <= qry, s, -jnp.inf)
        return s

    def fold(s, carry, c0):
        m, l, acc = carry
        width = s.shape[0]
        m_new = jnp.maximum(m, jnp.max(s, axis=0, keepdims=True))
        alpha = jnp.exp2((m - m_new) * qk_scale)
        p = jnp.exp2((s - m_new) * qk_scale)
        l = alpha * l + jnp.sum(p, axis=0, keepdims=True)
        p = p.astype(BF16)
        pv = jnp.dot(vt_ref[c0], p[:bq, :], preferred_element_type=F32)
        for t in range(1, width // bq):
            pv = pv + jnp.dot(vt_ref[c0 + t], p[t * bq:(t + 1) * bq, :],
                              preferred_element_type=F32)
        return m_new, l, alpha * acc + pv

    tiles = []
    for mi in range(n_pairs):
        tiles += [(2 * mi, 2 * mi, bq, True), (2 * mi + 1, 2 * mi, bk, True)]
        for t in range(mi):
            tiles += [(2 * mi, 2 * t, bk, False), (2 * mi + 1, 2 * t, bk, False)]
    remaining = {}
    for blk, _, _, _ in tiles:
        remaining[blk] = remaining.get(blk, 0) + 1

    q_aug, state = {}, {}

    def issue(tile):
        blk, c0, width, masked = tile
        if blk not in q_aug:
            q0 = blk * bq
            q_aug[blk] = jnp.concatenate([q_ref[0, q0:q0 + bq, :], eq_ref[0, 0, q0:q0 + bq, :]],
                                         axis=1)
            state[blk] = (jnp.full((1, bq), -jnp.inf, F32), jnp.zeros((1, bq), F32),
                          jnp.zeros((dh, bq), F32))
        return scores(q_aug[blk], c0, blk * bq, width, masked)

    def retire(tile, s):
        blk, c0, _, _ = tile
        state[blk] = fold(s, state[blk], c0)
        remaining[blk] -= 1
        if remaining[blk] == 0:
            _, l, acc = state.pop(blk)
            q_aug.pop(blk)
            o_ref[blk * bq:(blk + 1) * bq, :] = (acc / l).T.astype(o_ref.dtype)

    pending = None
    for tile in tiles:
        s = issue(tile)
        if pending is not None:
            retire(*pending)
        pending = (tile, s)
    retire(*pending)


def _fox_attention(qkv, eq, ek, batch, seq, n_heads, dh, side=()):
    assert dh == LANES and seq % FOX_BK == 0 and qkv.shape == (3 * n_heads, batch * seq, dh)
    nq = seq // FOX_BQ
    kern = functools.partial(_fox_kernel, n_side=len(side), n_pairs=nq // 2,
                             qk_scale=dh ** -0.5 * LOG2E)
    feat = pl.BlockSpec((1, 1, seq, LANES), lambda b, h: (b, h, 0, 0))
    side_specs = _side_specs(side, batch * n_heads, lambda b, h: b * n_heads + h)
    outs = pl.pallas_call(
        kern,
        grid=(batch, n_heads),
        in_specs=[pl.BlockSpec((1, seq, dh), lambda b, h: (h, b, 0)),
                  pl.BlockSpec((1, seq, dh), lambda b, h: (n_heads + h, b, 0)),
                  pl.BlockSpec((1, seq, dh), lambda b, h: (2 * n_heads + h, b, 0)),
                  feat, feat] + side_specs,
        out_specs=[pl.BlockSpec((seq, dh), lambda b, h: (b, h))] + side_specs,
        out_shape=[jax.ShapeDtypeStruct((batch * seq, n_heads * dh), BF16)]
        + [jax.ShapeDtypeStruct(w.shape, BF16) for w in side],
        scratch_shapes=[pltpu.VMEM((nq, dh, FOX_BQ), BF16)],
        compiler_params=_cparams(("parallel", "arbitrary"), 48),
        name="fox_attention",
    )(qkv, qkv, qkv, eq, ek, *side)
    return outs if side else outs[0]


def _t5_bucket(dist, num_buckets):
    max_exact = num_buckets // 2
    small = dist < max_exact
    large = max_exact + (np.log(np.maximum(dist, 1) / max_exact) / np.log(MAX_DISTANCE / max_exact)
                         * (num_buckets - max_exact)).astype(np.int64)
    large = np.minimum(large, num_buckets - 1)
    return np.where(small, dist, large)


def _bias_table_kernel(bucket_ref, rb_ref, o_ref, *, num_buckets):
    first = pl.program_id(0) == 0
    h = pl.program_id(1)
    bkt = bucket_ref[...]
    bias = jnp.zeros(bkt.shape, F32)
    for b in range(num_buckets):
        bias = jnp.where(bkt == b, rb_ref[b, h], bias)
    key = lax.broadcasted_iota(jnp.int32, bkt.shape, 0)
    qry = lax.broadcasted_iota(jnp.int32, bkt.shape, 1)
    dist = qry + BLOCK - key
    visible = (dist >= 0) & (dist < WINDOW) & ((key >= BLOCK) | jnp.logical_not(first))
    o_ref[0, 0] = jnp.where(visible, bias * LOG2E, -jnp.inf)


def _bias_table(rel_bias, n_heads):
    num_buckets = rel_bias.shape[0]
    ql = np.arange(BLOCK)[None, :]
    kl = np.arange(2 * BLOCK)[:, None]
    bucket = _t5_bucket(np.clip(ql + BLOCK - kl, 0, None), num_buckets).astype(np.int32)
    return pl.pallas_call(
        functools.partial(_bias_table_kernel, num_buckets=num_buckets),
        grid=(2, n_heads),
        in_specs=[pl.BlockSpec((2 * BLOCK, BLOCK), lambda f, h: (0, 0)),
                  pl.BlockSpec(memory_space=pltpu.SMEM)],
        out_specs=pl.BlockSpec((1, 1, 2 * BLOCK, BLOCK), lambda f, h: (f, h, 0, 0)),
        out_shape=jax.ShapeDtypeStruct((2, n_heads, 2 * BLOCK, BLOCK), F32),
        compiler_params=_cparams(("arbitrary", "arbitrary"), 32),
        name="swa_bias_table",
    )(jnp.asarray(bucket), rel_bias.astype(F32))


def _swa_kernel(q_ref, kp_ref, kc_ref, vp_ref, vc_ref, bias_ref, sink_ref, o_ref, *,
                n_kv, groups, dh, qk_scale):
    qt = q_ref[...].T
    k_band = jnp.concatenate([kp_ref[...], kc_ref[...]], axis=0)
    vt = jnp.concatenate([vp_ref[...], vc_ref[...]], axis=0).T
    outs = []
    for hk in range(n_kv):
        k = k_band[:, hk * dh:(hk + 1) * dh]
        q_grp = jnp.concatenate(
            [qt[(hk * groups + g) * dh:(hk * groups + g + 1) * dh, :] for g in range(groups)],
            axis=1)
        s = jnp.dot(k, q_grp, preferred_element_type=F32) * qk_scale
        probs, denoms = [], []
        for g in range(groups):
            hq = hk * groups + g
            sg = s[:, g * BLOCK:(g + 1) * BLOCK] + bias_ref[0, hq]
            sink = jnp.full((1, BLOCK), sink_ref[hq], F32) * LOG2E
            m = jnp.maximum(jnp.max(sg, axis=0, keepdims=True), sink)
            p = jnp.exp2(sg - m)
            denoms.append(jnp.sum(p, axis=0, keepdims=True) + jnp.exp2(sink - m))
            probs.append(p.astype(BF16))
        o = jnp.dot(vt[hk * dh:(hk + 1) * dh, :], jnp.concatenate(probs, axis=1),
                    preferred_element_type=F32)
        for g in range(groups):
            outs.append((o[:, g * BLOCK:(g + 1) * BLOCK] / denoms[g]).astype(o_ref.dtype))
    o_ref[...] = jnp.concatenate(outs, axis=0).T


def _swa_attention(qkv, bias, sinks, batch, seq, n_heads, n_kv, dh, q_col0):
    nb = seq // BLOCK
    qw = n_heads * dh
    kw = n_kv * dh
    assert q_col0 % qw == 0 and (q_col0 + qw) % kw == 0
    qb = q_col0 // qw
    kb = (q_col0 + qw) // kw
    vb = kb + 1
    prev = lambda b, i: b * nb + jnp.maximum(i - 1, 0)
    cur = lambda b, i: b * nb + i
    kern = functools.partial(_swa_kernel, n_kv=n_kv, groups=n_heads // n_kv, dh=dh,
                             qk_scale=dh ** -0.5 * LOG2E)
    return pl.pallas_call(
        kern,
        grid=(batch, nb),
        in_specs=[pl.BlockSpec((BLOCK, qw), lambda b, i: (cur(b, i), qb)),
                  pl.BlockSpec((BLOCK, kw), lambda b, i: (prev(b, i), kb)),
                  pl.BlockSpec((BLOCK, kw), lambda b, i: (cur(b, i), kb)),
                  pl.BlockSpec((BLOCK, kw), lambda b, i: (prev(b, i), vb)),
                  pl.BlockSpec((BLOCK, kw), lambda b, i: (cur(b, i), vb)),
                  pl.BlockSpec((1, n_heads, 2 * BLOCK, BLOCK),
                               lambda b, i: (jnp.minimum(i, 1), 0, 0, 0)),
                  pl.BlockSpec(memory_space=pltpu.SMEM)],
        out_specs=pl.BlockSpec((BLOCK, qw), lambda b, i: (cur(b, i), 0)),
        out_shape=jax.ShapeDtypeStruct((batch * seq, qw), BF16),
        compiler_params=_cparams(("parallel", "arbitrary"), 40),
        name="swa_attention",
    )(qkv, qkv, qkv, qkv, qkv, bias, sinks.astype(F32))


def kernel(x, norm1_g, w_in, b_forget, attn_sinks, rel_bias, w_branch_a, w_branch_b, w_out,
           norm2_g, w_ffn_gate, w_ffn_up, w_ffn_down, final_g):
    batch, seq, d = x.shape
    depth = w_in.shape[0]
    n_ha = b_forget.shape[1]
    n_hb = attn_sinks.shape[1]
    wa = w_branch_a.shape[1]
    wb = w_branch_b.shape[1]
    dh_a = wa // n_ha
    dh_b = wb // n_hb
    w_kvb = (w_in.shape[2] - 3 * wa - n_ha - wb - 2 * d) // 2
    n_kvb = w_kvb // dh_b
    c_f = 3 * wa
    c_qb = c_f + n_ha
    c_ga = c_qb + wb + 2 * w_kvb
    c_gb = c_ga + d

    xf = x.reshape(batch * seq, d)
    bias_tab = _bias_table(rel_bias, n_hb)

    for l in range(depth):
        w = w_in[l].astype(BF16)
        w_qkvb = w[:, c_qb:c_ga]
        w_ga = w[:, c_ga:c_gb]
        w_gb = w[:, c_gb:]
        w_f = jnp.pad(w[:, c_f:c_qb], ((0, 0), (0, LANES - n_ha)))
        b_f = jnp.pad(b_forget[l].astype(F32), (0, LANES - n_ha)).reshape(1, LANES)

        h1 = _rmsnorm(xf, norm1_g[l], BF16, "rmsnorm1")
        qkv_a = _matmul([h1], [w], [(0, 0)], [], _ep_identity, BF16, 2048, 512, n=c_f,
                        head_dim=dh_a, vmem_mib=56, name="in_proj_fox")
        qkv_b = _matmul([h1], [w_qkvb], [(0, 0)], [], _ep_identity, BF16, 2048, 512,
                        vmem_mib=56, name="in_proj_swa")
        eq, ek = _cumlog(h1, w_f, b_f, batch, seq, n_ha, dh_a)
        oa, w_out_b, w_bra, w_brb = _fox_attention(
            qkv_a, eq, ek, batch, seq, n_ha, dh_a, side=(w_out[l], w_branch_a[l], w_branch_b[l]))
        ob = _swa_attention(qkv_b, bias_tab, attn_sinks[l], batch, seq, n_hb, n_kvb, dh_b, 0)
        mixed, w_gate_b, w_up_b = _matmul(
            [h1, oa, ob], [w_ga, w_gb, w_bra, w_brb],
            [(0, 0), (0, 1), (1, 2), (2, 3)], [], _ep_gated_merge, BF16, 1024, 256,
            side=(w_ffn_gate[l], w_ffn_up[l]), vmem_mib=60, name="gated_merge")
        xf = _matmul([mixed], [w_out_b], [(0, 0)], [xf], _ep_residual, F32,
                     1024, 1024, vmem_mib=58, name="out_proj")

        h2 = _rmsnorm(xf, norm2_g[l], BF16, "rmsnorm2")
        hidden, w_down_b = _matmul([h2], [w_gate_b, w_up_b], [(0, 0), (0, 1)], [], _ep_swiglu,
                                   BF16, 2048, 256, side=(w_ffn_down[l],),
                                   row_chunk=MM_ROW_CHUNK, vmem_mib=56, name="ffn_gate_up")
        xf = _matmul([hidden], [w_down_b], [(0, 0)], [xf], _ep_residual, F32,
                     512, 1024, n_outer=True, vmem_mib=60, name="ffn_down")

    out = _rmsnorm(xf, final_g, x.dtype, "rmsnorm_final")
    return out.reshape(batch, seq, d)
```

```python
import functools

import numpy as np
import jax
import jax.numpy as jnp
from jax import lax
from jax.experimental import pallas as pl
from jax.experimental.pallas import tpu as pltpu

F32 = jnp.float32
BF16 = jnp.bfloat16

EPS = 1e-6
WINDOW = 128
BLOCK = 128
MAX_DISTANCE = 128
LANES = 128
BF16_ROWS = 16
MM_ROW_CHUNK = 512
MIB = 1024 * 1024
LOG2E = 1.4426950408889634


def _cparams(semantics, vmem_mib):
    return pltpu.CompilerParams(dimension_semantics=semantics,
                                vmem_limit_bytes=vmem_mib * MIB)


def _rmsnorm_kernel(x_ref, g_ref, o_ref):
    x = x_ref[...].astype(F32)
    ms = jnp.mean(x * x, axis=-1, keepdims=True)
    o_ref[...] = (x * lax.rsqrt(ms + EPS) * g_ref[...]).astype(o_ref.dtype)


def _rmsnorm(x, g, out_dtype, name, tm=256):
    m, d = x.shape
    return pl.pallas_call(
        _rmsnorm_kernel,
        grid=(m // tm,),
        in_specs=[pl.BlockSpec((tm, d), lambda i: (i, 0)),
                  pl.BlockSpec((1, d), lambda i: (0, 0))],
        out_specs=pl.BlockSpec((tm, d), lambda i: (i, 0)),
        out_shape=jax.ShapeDtypeStruct((m, d), out_dtype),
        compiler_params=_cparams(("parallel",), 40),
        name=name,
    )(x, g.reshape(1, d).astype(F32))


def _cast_kernel(x_ref, o_ref):
    o_ref[...] = x_ref[...].astype(o_ref.dtype)


def _cast_bf16(w, rows=64):
    r, c = w.shape
    assert r % rows == 0 and rows % BF16_ROWS == 0
    return pl.pallas_call(
        _cast_kernel,
        grid=(r // rows,),
        in_specs=[pl.BlockSpec((rows, c), lambda i: (i, 0))],
        out_specs=pl.BlockSpec((rows, c), lambda i: (i, 0)),
        out_shape=jax.ShapeDtypeStruct((r, c), BF16),
        compiler_params=_cparams(("parallel",), 40),
        name="cast_w_in",
    )(w)


def _mm_kernel(*refs, n_a, n_b, pairs, n_extra, n_side, n_out, acc_outs, row_chunk, epilogue):
    n_in = n_a + n_b + n_extra + n_side
    a_refs = refs[:n_a]
    b_refs = refs[n_a:n_a + n_b]
    e_refs = refs[n_a + n_b:n_a + n_b + n_extra]
    side_in = refs[n_a + n_b + n_extra:n_in]
    o_refs = refs[n_in:n_in + n_out]
    side_out = refs[n_in + n_out:n_in + n_out + n_side]
    bm = o_refs[0].shape[-2]

    def extra_block(e_ref, rows):
        return e_ref[...] if e_ref.shape[0] == 1 else e_ref[rows, :]

    def finish(rows, accs):
        results = epilogue(accs, [extra_block(e, rows) for e in e_refs])
        for k, (o_ref, r) in enumerate(zip(o_refs, results if n_out > 1 else (results,))):
            if k in acc_outs:
                first = pl.program_id(1) == 0

                @pl.when(first)
                def _(o_ref=o_ref, r=r):
                    o_ref[rows, :] = r.astype(o_ref.dtype)

                @pl.when(jnp.logical_not(first))
                def _(o_ref=o_ref, r=r):
                    o_ref[rows, :] += r.astype(o_ref.dtype)
            elif len(o_ref.shape) == 3:
                dh = o_ref.shape[2]
                for t in range(o_ref.shape[0]):
                    o_ref[t, rows, :] = r[:, t * dh:(t + 1) * dh].astype(o_ref.dtype)
            else:
                o_ref[rows, :] = r.astype(o_ref.dtype)

    pending = None
    for r in range(bm // row_chunk):
        rows = pl.ds(r * row_chunk, row_chunk)
        accs = [jnp.dot(a_refs[ai][rows, :], b_refs[bi][...], preferred_element_type=F32)
                for ai, bi in pairs]
        if pending is not None:
            finish(*pending)
        pending = (rows, accs)
    finish(*pending)
    for src_ref, dst_ref in zip(side_in, side_out):
        dst_ref[...] = src_ref[...].astype(dst_ref.dtype)


def _side_specs(side, n_steps, step_index):
    specs = []
    for w in side:
        rows = w.shape[0] // n_steps
        assert rows * n_steps == w.shape[0] and rows % BF16_ROWS == 0
        specs.append(pl.BlockSpec((rows, w.shape[1]), lambda *g: (step_index(*g), 0)))
    return specs


def _matmul(a_list, b_list, pairs, extras, epilogue, out_dtype, bm, bn, *,
            n=None, n_outer=False, side=(), more_outs=(), row_chunk=None, head_dim=None,
            vmem_mib=48, name):
    b_list = [b if isinstance(b, tuple) else (b, 0) for b in b_list]
    row_chunk = bm if row_chunk is None else row_chunk
    assert bm % row_chunk == 0
    m = a_list[0].shape[0]
    n = b_list[0][0].shape[1] - b_list[0][1] if n is None else n
    assert m % bm == 0 and n % bn == 0
    if n_outer:
        grid = (n // bn, m // bm)
        mi = lambda j, i: i
        nj = lambda j, i: j
    else:
        grid = (m // bm, n // bn)
        mi = lambda i, j: i
        nj = lambda i, j: j
    in_specs = []
    for a in a_list:
        assert a.shape[0] == m
        in_specs.append(pl.BlockSpec((bm, a.shape[1]), lambda *g: (mi(*g), 0)))
    for b, col0 in b_list:
        assert col0 % bn == 0 and b.shape[1] >= col0 + n
        index = lambda *g, off=col0 // bn: (0, off + nj(*g))
        if n_outer:
            in_specs.append(pl.BlockSpec((b.shape[0], bn), index, pipeline_mode=pl.Buffered(1)))
        else:
            in_specs.append(pl.BlockSpec((b.shape[0], bn), index))
    tile_spec = pl.BlockSpec((bm, bn), lambda *g: (mi(*g), nj(*g)))
    for e in extras:
        if e.shape[0] == 1:
            in_specs.append(pl.BlockSpec((1, bn), lambda *g: (0, nj(*g))))
        elif e.shape == (m, n):
            in_specs.append(tile_spec)
        else:
            assert e.shape[0] == m
            in_specs.append(pl.BlockSpec((bm, e.shape[1]), lambda *g: (mi(*g), 0)))
    out_specs, out_shapes, acc_outs = [tile_spec], [jax.ShapeDtypeStruct((m, n), out_dtype)], []
    if head_dim is not None:
        assert bn % head_dim == 0 and not more_outs
        out_specs = [pl.BlockSpec((bn // head_dim, bm, head_dim), lambda *g: (nj(*g), mi(*g), 0))]
        out_shapes = [jax.ShapeDtypeStruct((n // head_dim, m, head_dim), out_dtype)]
    for dtype, kind in more_outs:
        if kind == "tile":
            out_specs.append(tile_spec)
            out_shapes.append(jax.ShapeDtypeStruct((m, n), dtype))
        else:
            assert kind == "rowacc" and not n_outer
            acc_outs.append(len(out_specs))
            out_specs.append(pl.BlockSpec((bm, LANES), lambda *g: (mi(*g), 0)))
            out_shapes.append(jax.ShapeDtypeStruct((m, LANES), dtype))
    side_specs = _side_specs(side, grid[0] * grid[1], lambda g0, g1: g0 * grid[1] + g1)
    kern = functools.partial(_mm_kernel, n_a=len(a_list), n_b=len(b_list), pairs=tuple(pairs),
                             n_extra=len(extras), n_side=len(side), n_out=len(out_specs),
                             acc_outs=tuple(acc_outs), row_chunk=row_chunk, epilogue=epilogue)
    outs = pl.pallas_call(
        kern,
        grid=grid,
        in_specs=in_specs + side_specs,
        out_specs=out_specs + side_specs,
        out_shape=out_shapes + [jax.ShapeDtypeStruct(w.shape, BF16) for w in side],
        compiler_params=_cparams(("parallel", "arbitrary"), vmem_mib),
        name=name,
    )(*a_list, *[b for b, _ in b_list], *extras, *side)
    return outs if len(outs) > 1 else outs[0]


def _ep_identity(accs, extras):
    return accs[0]


def _ep_residual(accs, extras):
    return extras[0] + accs[0]


def _ep_gated_merge(accs, extras):
    ga, gb, ya, yb = accs
    return jax.nn.sigmoid(ga) * ya + jax.nn.sigmoid(gb) * yb


def _ep_swiglu(accs, extras):
    g, u = accs
    return (g * jax.nn.sigmoid(g)) * u


N_SPLIT = 3


def _cumlog_kernel(h_ref, wf_ref, bf_ref, place_ref, *rest, n_side, ts, n_heads, inv_scale):
    side_in = rest[:n_side]
    eq_ref, ek_ref = rest[n_side:n_side + 2]
    side_out, carry_ref = rest[n_side + 2:2 * n_side + 2], rest[2 * n_side + 2]
    for src_ref, dst_ref in zip(side_in, side_out):
        dst_ref[...] = src_ref[...].astype(dst_ref.dtype)
    t = pl.program_id(1)

    @pl.when(t == 0)
    def _():
        carry_ref[...] = jnp.zeros_like(carry_ref)

    f = jnp.dot(h_ref[...], wf_ref[...], preferred_element_type=F32) + bf_ref[...]
    log_f = jnp.minimum(f, 0.0) - jnp.log(1.0 + jnp.exp(-jnp.abs(f)))
    row = lax.broadcasted_iota(jnp.int32, (ts, ts), 0)
    col = lax.broadcasted_iota(jnp.int32, (ts, ts), 1)
    tri = (col <= row).astype(F32)
    c = jnp.dot(tri, log_f, precision=lax.Precision.HIGHEST,
                preferred_element_type=F32) + carry_ref[...]
    carry_ref[...] = c[ts - 1:ts, :]
    c3 = c * inv_scale
    hi = c3.astype(BF16).astype(F32)
    mid = (c3 - hi).astype(BF16).astype(F32)
    lo = (c3 - hi - mid).astype(BF16).astype(F32)
    lane = lax.broadcasted_iota(jnp.int32, (ts, LANES), 1)
    packed = jnp.where(
        lane < n_heads, hi,
        jnp.where(lane < 2 * n_heads, pltpu.roll(mid, n_heads, 1),
                  jnp.where(lane < N_SPLIT * n_heads, pltpu.roll(lo, 2 * n_heads, 1),
                            jnp.where(lane == N_SPLIT * n_heads, 1.0, 0.0))))
    feats = jnp.dot(packed.astype(BF16), place_ref[...], preferred_element_type=F32)
    for h in range(n_heads):
        eq_ref[0, h] = feats[:, h * LANES:(h + 1) * LANES].astype(BF16)
        ek_ref[0, h] = feats[:, (n_heads + h) * LANES:(n_heads + h + 1) * LANES].astype(BF16)


def _placement(n_heads):
    p = np.zeros((LANES, 2 * n_heads * LANES), np.float32)
    one = N_SPLIT * n_heads
    for h in range(n_heads):
        for j in range(N_SPLIT):
            p[j * n_heads + h, h * LANES + j] = 1.0
            p[j * n_heads + h, (n_heads + h) * LANES + N_SPLIT + j] = -1.0
            p[one, h * LANES + N_SPLIT + j] = 1.0
            p[one, (n_heads + h) * LANES + j] = 1.0
    return jnp.asarray(p, BF16)


def _cumlog(h1, w_f, b_f, batch, seq, n_heads, dh, ts=512, side=()):
    assert (N_SPLIT * n_heads) < LANES
    d = h1.shape[1]
    nt = seq // ts
    feat = pl.BlockSpec((1, n_heads, ts, LANES), lambda b, t: (b, 0, t, 0))
    side_specs = _side_specs(side, batch * nt, lambda b, t: b * nt + t)
    return pl.pallas_call(
        functools.partial(_cumlog_kernel, n_side=len(side), ts=ts, n_heads=n_heads,
                          inv_scale=dh ** 0.5),
        grid=(batch, nt),
        in_specs=[pl.BlockSpec((ts, d), lambda b, t: (b * nt + t, 0)),
                  pl.BlockSpec((d, LANES), lambda b, t: (0, 0)),
                  pl.BlockSpec((1, LANES), lambda b, t: (0, 0)),
                  pl.BlockSpec((LANES, 2 * n_heads * LANES), lambda b, t: (0, 0))] + side_specs,
        out_specs=[feat, feat] + side_specs,
        out_shape=[jax.ShapeDtypeStruct((batch, n_heads, seq, LANES), BF16)] * 2
        + [jax.ShapeDtypeStruct(w.shape, BF16) for w in side],
        scratch_shapes=[pltpu.VMEM((1, LANES), F32)],
        compiler_params=_cparams(("parallel", "arbitrary"), 48),
        name="fox_cumlog",
    )(h1, w_f, b_f, _placement(n_heads), *side)


FOX_BQ = 512
FOX_BK = 2 * FOX_BQ


def _fox_kernel(q_ref, k_ref, v_ref, eq_ref, ek_ref, *rest, n_side, n_pairs, qk_scale):
    bq, bk = FOX_BQ, FOX_BK
    dh = q_ref.shape[2]
    side_in, o_ref = rest[:n_side], rest[n_side]
    side_out, vt_ref = rest[n_side + 1:2 * n_side + 1], rest[2 * n_side + 1]
    for src_ref, dst_ref in zip(side_in, side_out):
        dst_ref[...] = src_ref[...].astype(dst_ref.dtype)

    for c in range(vt_ref.shape[0]):
        vt_ref[c] = v_ref[0, c * bq:(c + 1) * bq, :].T

    def scores(q, c0, q0, width, masked):
        k0 = c0 * bq
        k = jnp.concatenate([k_ref[0, k0:k0 + width, :], ek_ref[0, 0, k0:k0 + width, :]], axis=1)
        s = lax.dot_general(k, q, (((1,), (1,)), ((), ())),
                            preferred_element_type=F32)
        if masked:
            key = k0 + lax.broadcasted_iota(jnp.int32, (width, bq), 0)
            qry = q0 + lax.broadcasted_iota(jnp.int32, (width, bq), 1)
            s = jnp.where(key <= qry, s, -jnp.inf)
        return s

    def fold(s, carry, c0):
        m, l, acc = carry
        width = s.shape[0]
        m_new = jnp.maximum(m, jnp.max(s, axis=0, keepdims=True))
        alpha = jnp.exp2((m - m_new) * qk_scale)
        p = jnp.exp2((s - m_new) * qk_scale)
        l = alpha * l + jnp.sum(p, axis=0, keepdims=True)
        p = p.astype(BF16)
        pv = jnp.dot(vt_ref[c0], p[:bq, :], preferred_element_type=F32)
        for t in range(1, width // bq):
            pv = pv + jnp.dot(vt_ref[c0 + t], p[t * bq:(t + 1) * bq, :],
                              preferred_element_type=F32)
        return m_new, l, alpha * acc + pv

    tiles = []
    for mi in range(n_pairs):
        tiles += [(2 * mi, 2 * mi, bq, True), (2 * mi + 1, 2 * mi, bk, True)]
        for t in range(mi):
            tiles += [(2 * mi, 2 * t, bk, False), (2 * mi + 1, 2 * t, bk, False)]
    remaining = {}
    for blk, _, _, _ in tiles:
        remaining[blk] = remaining.get(blk, 0) + 1

    q_aug, state = {}, {}

    def issue(tile):
        blk, c0, width, masked = tile
        if blk not in q_aug:
            q0 = blk * bq
            q_aug[blk] = jnp.concatenate([q_ref[0, q0:q0 + bq, :], eq_ref[0, 0, q0:q0 + bq, :]],
                                         axis=1)
            state[blk] = (jnp.full((1, bq), -jnp.inf, F32), jnp.zeros((1, bq), F32),
                          jnp.zeros((dh, bq), F32))
        return scores(q_aug[blk], c0, blk * bq, width, masked)

    def retire(tile, s):
        blk, c0, _, _ = tile
        state[blk] = fold(s, state[blk], c0)
        remaining[blk] -= 1
        if remaining[blk] == 0:
            _, l, acc = state.pop(blk)
            q_aug.pop(blk)
            o_ref[blk * bq:(blk + 1) * bq, :] = (acc / l).T.astype(o_ref.dtype)

    pending = None
    for tile in tiles:
        s = issue(tile)
        if pending is not None:
            retire(*pending)
        pending = (tile, s)
    retire(*pending)


def _fox_attention(qkv, eq, ek, batch, seq, n_heads, dh, side=()):
    assert dh == LANES and seq % FOX_BK == 0 and qkv.shape == (3 * n_heads, batch * seq, dh)
    nq = seq // FOX_BQ
    kern = functools.partial(_fox_kernel, n_side=len(side), n_pairs=nq // 2,
                             qk_scale=dh ** -0.5 * LOG2E)
    feat = pl.BlockSpec((1, 1, seq, LANES), lambda b, h: (b, h, 0, 0))
    side_specs = _side_specs(side, batch * n_heads, lambda b, h: b * n_heads + h)
    outs = pl.pallas_call(
        kern,
        grid=(batch, n_heads),
        in_specs=[pl.BlockSpec((1, seq, dh), lambda b, h: (h, b, 0)),
                  pl.BlockSpec((1, seq, dh), lambda b, h: (n_heads + h, b, 0)),
                  pl.BlockSpec((1, seq, dh), lambda b, h: (2 * n_heads + h, b, 0)),
                  feat, feat] + side_specs,
        out_specs=[pl.BlockSpec((seq, dh), lambda b, h: (b, h))] + side_specs,
        out_shape=[jax.ShapeDtypeStruct((batch * seq, n_heads * dh), BF16)]
        + [jax.ShapeDtypeStruct(w.shape, BF16) for w in side],
        scratch_shapes=[pltpu.VMEM((nq, dh, FOX_BQ), BF16)],
        compiler_params=_cparams(("parallel", "arbitrary"), 48),
        name="fox_attention",
    )(qkv, qkv, qkv, eq, ek, *side)
    return outs if side else outs[0]


def _t5_bucket(dist, num_buckets):
    max_exact = num_buckets // 2
    small = dist < max_exact
    large = max_exact + (np.log(np.maximum(dist, 1) / max_exact) / np.log(MAX_DISTANCE / max_exact)
                         * (num_buckets - max_exact)).astype(np.int64)
    large = np.minimum(large, num_buckets - 1)
    return np.where(small, dist, large)


def _bias_table_kernel(bucket_ref, rb_ref, o_ref, *, num_buckets):
    first = pl.program_id(0) == 0
    h = pl.program_id(1)
    bkt = bucket_ref[...]
    bias = jnp.zeros(bkt.shape, F32)
    for b in range(num_buckets):
        bias = jnp.where(bkt == b, rb_ref[b, h], bias)
    key = lax.broadcasted_iota(jnp.int32, bkt.shape, 0)
    qry = lax.broadcasted_iota(jnp.int32, bkt.shape, 1)
    dist = qry + BLOCK - key
    visible = (dist >= 0) & (dist < WINDOW) & ((key >= BLOCK) | jnp.logical_not(first))
    o_ref[0, 0] = jnp.where(visible, bias * LOG2E, -jnp.inf)


def _bias_table(rel_bias, n_heads):
    num_buckets = rel_bias.shape[0]
    ql = np.arange(BLOCK)[None, :]
    kl = np.arange(2 * BLOCK)[:, None]
    bucket = _t5_bucket(np.clip(ql + BLOCK - kl, 0, None), num_buckets).astype(np.int32)
    return pl.pallas_call(
        functools.partial(_bias_table_kernel, num_buckets=num_buckets),
        grid=(2, n_heads),
        in_specs=[pl.BlockSpec((2 * BLOCK, BLOCK), lambda f, h: (0, 0)),
                  pl.BlockSpec(memory_space=pltpu.SMEM)],
        out_specs=pl.BlockSpec((1, 1, 2 * BLOCK, BLOCK), lambda f, h: (f, h, 0, 0)),
        out_shape=jax.ShapeDtypeStruct((2, n_heads, 2 * BLOCK, BLOCK), F32),
        compiler_params=_cparams(("arbitrary", "arbitrary"), 32),
        name="swa_bias_table",
    )(jnp.asarray(bucket), rel_bias.astype(F32))


def _swa_kernel(q_ref, kp_ref, kc_ref, vp_ref, vc_ref, bias_ref, sink_ref, o_ref, *,
                n_kv, groups, dh, qk_scale):
    qt = q_ref[...].T
    k_band = jnp.concatenate([kp_ref[...], kc_ref[...]], axis=0)
    vt = jnp.concatenate([vp_ref[...], vc_ref[...]], axis=0).T
    outs = []
    for hk in range(n_kv):
        k = k_band[:, hk * dh:(hk + 1) * dh]
        q_grp = jnp.concatenate(
            [qt[(hk * groups + g) * dh:(hk * groups + g + 1) * dh, :] for g in range(groups)],
            axis=1)
        s = jnp.dot(k, q_grp, preferred_element_type=F32) * qk_scale
        probs, denoms = [], []
        for g in range(groups):
            hq = hk * groups + g
            sg = s[:, g * BLOCK:(g + 1) * BLOCK] + bias_ref[0, hq]
            sink = jnp.full((1, BLOCK), sink_ref[hq], F32) * LOG2E
            m = jnp.maximum(jnp.max(sg, axis=0, keepdims=True), sink)
            p = jnp.exp2(sg - m)
            denoms.append(jnp.sum(p, axis=0, keepdims=True) + jnp.exp2(sink - m))
            probs.append(p.astype(BF16))
        o = jnp.dot(vt[hk * dh:(hk + 1) * dh, :], jnp.concatenate(probs, axis=1),
                    preferred_element_type=F32)
        for g in range(groups):
            outs.append((o[:, g * BLOCK:(g + 1) * BLOCK] / denoms[g]).astype(o_ref.dtype))
    o_ref[...] = jnp.concatenate(outs, axis=0).T


def _swa_attention(qkv, bias, sinks, batch, seq, n_heads, n_kv, dh, q_col0):
    nb = seq // BLOCK
    qw = n_heads * dh
    kw = n_kv * dh
    assert q_col0 % qw == 0 and (q_col0 + qw) % kw == 0
    qb = q_col0 // qw
    kb = (q_col0 + qw) // kw
    vb = kb + 1
    prev = lambda b, i: b * nb + jnp.maximum(i - 1, 0)
    cur = lambda b, i: b * nb + i
    kern = functools.partial(_swa_kernel, n_kv=n_kv, groups=n_heads // n_kv, dh=dh,
                             qk_scale=dh ** -0.5 * LOG2E)
    return pl.pallas_call(
        kern,
        grid=(batch, nb),
        in_specs=[pl.BlockSpec((BLOCK, qw), lambda b, i: (cur(b, i), qb)),
                  pl.BlockSpec((BLOCK, kw), lambda b, i: (prev(b, i), kb)),
                  pl.BlockSpec((BLOCK, kw), lambda b, i: (cur(b, i), kb)),
                  pl.BlockSpec((BLOCK, kw), lambda b, i: (prev(b, i), vb)),
                  pl.BlockSpec((BLOCK, kw), lambda b, i: (cur(b, i), vb)),
                  pl.BlockSpec((1, n_heads, 2 * BLOCK, BLOCK),
                               lambda b, i: (jnp.minimum(i, 1), 0, 0, 0)),
                  pl.BlockSpec(memory_space=pltpu.SMEM)],
        out_specs=pl.BlockSpec((BLOCK, qw), lambda b, i: (cur(b, i), 0)),
        out_shape=jax.ShapeDtypeStruct((batch * seq, qw), BF16),
        compiler_params=_cparams(("parallel", "arbitrary"), 40),
        name="swa_attention",
    )(qkv, qkv, qkv, qkv, qkv, bias, sinks.astype(F32))


def kernel(x, norm1_g, w_in, b_forget, attn_sinks, rel_bias, w_branch_a, w_branch_b, w_out,
           norm2_g, w_ffn_gate, w_ffn_up, w_ffn_down, final_g):
    batch, seq, d = x.shape
    depth = w_in.shape[0]
    n_ha = b_forget.shape[1]
    n_hb = attn_sinks.shape[1]
    wa = w_branch_a.shape[1]
    wb = w_branch_b.shape[1]
    dh_a = wa // n_ha
    dh_b = wb // n_hb
    w_kvb = (w_in.shape[2] - 3 * wa - n_ha - wb - 2 * d) // 2
    n_kvb = w_kvb // dh_b
    c_f = 3 * wa
    c_qb = c_f + n_ha
    c_ga = c_qb + wb + 2 * w_kvb
    c_gb = c_ga + d

    xf = x.reshape(batch * seq, d)
    bias_tab = _bias_table(rel_bias, n_hb)

    for l in range(depth):
        w = _cast_bf16(w_in[l])
        w_qkvb = w[:, c_qb:c_ga]
        w_ga = w[:, c_ga:c_gb]
        w_gb = w[:, c_gb:]
        w_f = jnp.pad(w[:, c_f:c_qb], ((0, 0), (0, LANES - n_ha)))
        b_f = jnp.pad(b_forget[l].astype(F32), (0, LANES - n_ha)).reshape(1, LANES)

        h1 = _rmsnorm(xf, norm1_g[l], BF16, "rmsnorm1")
        qkv_a = _matmul([h1], [w], [(0, 0)], [], _ep_identity, BF16, 2048, 512, n=c_f,
                        head_dim=dh_a, vmem_mib=56, name="in_proj_fox")
        qkv_b = _matmul([h1], [w_qkvb], [(0, 0)], [], _ep_identity, BF16, 2048, 512,
                        vmem_mib=56, name="in_proj_swa")
        eq, ek = _cumlog(h1, w_f, b_f, batch, seq, n_ha, dh_a)
        oa, w_out_b, w_bra, w_brb = _fox_attention(
            qkv_a, eq, ek, batch, seq, n_ha, dh_a, side=(w_out[l], w_branch_a[l], w_branch_b[l]))
        ob = _swa_attention(qkv_b, bias_tab, attn_sinks[l], batch, seq, n_hb, n_kvb, dh_b, 0)
        mixed, w_gate_b, w_up_b = _matmul(
            [h1, oa, ob], [w_ga, w_gb, w_bra, w_brb],
            [(0, 0), (0, 1), (1, 2), (2, 3)], [], _ep_gated_merge, BF16, 1024, 256,
            side=(w_ffn_gate[l], w_ffn_up[l]), row_chunk=MM_ROW_CHUNK, vmem_mib=60,
            name="gated_merge")
        xf = _matmul([mixed], [w_out_b], [(0, 0)], [xf], _ep_residual, F32,
                     1024, 1024, row_chunk=MM_ROW_CHUNK, vmem_mib=58, name="out_proj")

        h2 = _rmsnorm(xf, norm2_g[l], BF16, "rmsnorm2")
        hidden, w_down_b = _matmul([h2], [w_gate_b, w_up_b], [(0, 0), (0, 1)], [], _ep_swiglu,
                                   BF16, 2048, 256, side=(w_ffn_down[l],),
                                   row_chunk=MM_ROW_CHUNK, vmem_mib=56, name="ffn_gate_up")
        xf = _matmul([hidden], [w_down_b], [(0, 0)], [xf], _ep_residual, F32,
                     512, 1024, n_outer=True, vmem_mib=60, name="ffn_down")

    out = _rmsnorm(xf, final_g, x.dtype, "rmsnorm_final")
    return out.reshape(batch, seq, d)
```

```python
import functools

import numpy as np
import jax
import jax.numpy as jnp
from jax import lax
from jax.experimental import pallas as pl
from jax.experimental.pallas import tpu as pltpu

F32 = jnp.float32
BF16 = jnp.bfloat16

EPS = 1e-6
WINDOW = 128
BLOCK = 128
MAX_DISTANCE = 128
LANES = 128
BF16_ROWS = 16
MM_ROW_CHUNK = 512
MIB = 1024 * 1024
LOG2E = 1.4426950408889634


def _cparams(semantics, vmem_mib):
    return pltpu.CompilerParams(dimension_semantics=semantics,
                                vmem_limit_bytes=vmem_mib * MIB)


def _rmsnorm_kernel(x_ref, g_ref, o_ref):
    x = x_ref[...].astype(F32)
    ms = jnp.mean(x * x, axis=-1, keepdims=True)
    o_ref[...] = (x * lax.rsqrt(ms + EPS) * g_ref[...]).astype(o_ref.dtype)


def _rmsnorm(x, g, out_dtype, name, tm=256):
    m, d = x.shape
    return pl.pallas_call(
        _rmsnorm_kernel,
        grid=(m // tm,),
        in_specs=[pl.BlockSpec((tm, d), lambda i: (i, 0)),
                  pl.BlockSpec((1, d), lambda i: (0, 0))],
        out_specs=pl.BlockSpec((tm, d), lambda i: (i, 0)),
        out_shape=jax.ShapeDtypeStruct((m, d), out_dtype),
        compiler_params=_cparams(("parallel",), 40),
        name=name,
    )(x, g.reshape(1, d).astype(F32))


def _cast_kernel(x_ref, o_ref):
    o_ref[...] = x_ref[...].astype(o_ref.dtype)


def _cast_rows_bf16(w, row0, n_rows, block_rows, name):
    c = w.shape[1]
    assert row0 % block_rows == 0 and n_rows % block_rows == 0 and block_rows % BF16_ROWS == 0
    first = row0 // block_rows
    return pl.pallas_call(
        _cast_kernel,
        grid=(n_rows // block_rows,),
        in_specs=[pl.BlockSpec((block_rows, c), lambda i: (first + i, 0))],
        out_specs=pl.BlockSpec((block_rows, c), lambda i: (i, 0)),
        out_shape=jax.ShapeDtypeStruct((n_rows, c), BF16),
        compiler_params=_cparams(("parallel",), 40),
        name=name,
    )(w)


def _mm_kernel(*refs, n_a, n_b, pairs, b_trans, n_extra, n_side, n_out, acc_outs, row_chunk,
               epilogue):
    n_in = n_a + n_b + n_extra + n_side
    a_refs = refs[:n_a]
    b_refs = refs[n_a:n_a + n_b]
    e_refs = refs[n_a + n_b:n_a + n_b + n_extra]
    side_in = refs[n_a + n_b + n_extra:n_in]
    o_refs = refs[n_in:n_in + n_out]
    side_out = refs[n_in + n_out:n_in + n_out + n_side]
    bm = o_refs[0].shape[-2]

    def extra_block(e_ref, rows):
        return e_ref[...] if e_ref.shape[0] == 1 else e_ref[rows, :]

    def finish(rows, accs):
        results = epilogue(accs, [extra_block(e, rows) for e in e_refs])
        for k, (o_ref, r) in enumerate(zip(o_refs, results if n_out > 1 else (results,))):
            if k in acc_outs:
                first = pl.program_id(1) == 0

                @pl.when(first)
                def _(o_ref=o_ref, r=r):
                    o_ref[rows, :] = r.astype(o_ref.dtype)

                @pl.when(jnp.logical_not(first))
                def _(o_ref=o_ref, r=r):
                    o_ref[rows, :] += r.astype(o_ref.dtype)
            elif len(o_ref.shape) == 3:
                dh = o_ref.shape[2]
                for t in range(o_ref.shape[0]):
                    o_ref[t, rows, :] = r[:, t * dh:(t + 1) * dh].astype(o_ref.dtype)
            else:
                o_ref[rows, :] = r.astype(o_ref.dtype)

    pending = None
    for r in range(bm // row_chunk):
        rows = pl.ds(r * row_chunk, row_chunk)
        accs = [lax.dot_general(a_refs[ai][rows, :], b_refs[bi][...],
                                (((1,), (1 if b_trans[bi] else 0,)), ((), ())),
                                preferred_element_type=F32)
                for ai, bi in pairs]
        if pending is not None:
            finish(*pending)
        pending = (rows, accs)
    finish(*pending)
    for src_ref, dst_ref in zip(side_in, side_out):
        dst_ref[...] = src_ref[...].astype(dst_ref.dtype)


def _side_specs(side, n_steps, step_index):
    specs = []
    for w in side:
        rows = w.shape[0] // n_steps
        assert rows * n_steps == w.shape[0] and rows % BF16_ROWS == 0
        specs.append(pl.BlockSpec((rows, w.shape[1]), lambda *g: (step_index(*g), 0)))
    return specs


def _matmul(a_list, b_list, pairs, extras, epilogue, out_dtype, bm, bn, *,
            n=None, n_outer=False, side=(), more_outs=(), row_chunk=None, head_dim=None,
            vmem_mib=48, name):
    b_list = [(tuple(b) + (False,))[:3] if isinstance(b, tuple) else (b, 0, False) for b in b_list]
    row_chunk = bm if row_chunk is None else row_chunk
    assert bm % row_chunk == 0
    m = a_list[0].shape[0]
    if n is None:
        b0, off0, t0 = b_list[0]
        n = b0.shape[0 if t0 else 1] - off0
    assert m % bm == 0 and n % bn == 0
    if n_outer:
        grid = (n // bn, m // bm)
        mi = lambda j, i: i
        nj = lambda j, i: j
    else:
        grid = (m // bm, n // bn)
        mi = lambda i, j: i
        nj = lambda i, j: j
    in_specs = []
    for a in a_list:
        assert a.shape[0] == m
        in_specs.append(pl.BlockSpec((bm, a.shape[1]), lambda *g: (mi(*g), 0)))
    for b, off, trans in b_list:
        assert off % bn == 0 and b.shape[0 if trans else 1] >= off + n
        if trans:
            assert not n_outer
            in_specs.append(pl.BlockSpec((bn, b.shape[1]),
                                         lambda *g, blk=off // bn: (blk + nj(*g), 0)))
            continue
        index = lambda *g, blk=off // bn: (0, blk + nj(*g))
        if n_outer:
            in_specs.append(pl.BlockSpec((b.shape[0], bn), index, pipeline_mode=pl.Buffered(1)))
        else:
            in_specs.append(pl.BlockSpec((b.shape[0], bn), index))
    tile_spec = pl.BlockSpec((bm, bn), lambda *g: (mi(*g), nj(*g)))
    for e in extras:
        if e.shape[0] == 1:
            in_specs.append(pl.BlockSpec((1, bn), lambda *g: (0, nj(*g))))
        elif e.shape == (m, n):
            in_specs.append(tile_spec)
        else:
            assert e.shape[0] == m
            in_specs.append(pl.BlockSpec((bm, e.shape[1]), lambda *g: (mi(*g), 0)))
    out_specs, out_shapes, acc_outs = [tile_spec], [jax.ShapeDtypeStruct((m, n), out_dtype)], []
    if head_dim is not None:
        assert bn % head_dim == 0 and not more_outs
        out_specs = [pl.BlockSpec((bn // head_dim, bm, head_dim), lambda *g: (nj(*g), mi(*g), 0))]
        out_shapes = [jax.ShapeDtypeStruct((n // head_dim, m, head_dim), out_dtype)]
    for dtype, kind in more_outs:
        if kind == "tile":
            out_specs.append(tile_spec)
            out_shapes.append(jax.ShapeDtypeStruct((m, n), dtype))
        else:
            assert kind == "rowacc" and not n_outer
            acc_outs.append(len(out_specs))
            out_specs.append(pl.BlockSpec((bm, LANES), lambda *g: (mi(*g), 0)))
            out_shapes.append(jax.ShapeDtypeStruct((m, LANES), dtype))
    side_specs = _side_specs(side, grid[0] * grid[1], lambda g0, g1: g0 * grid[1] + g1)
    kern = functools.partial(_mm_kernel, n_a=len(a_list), n_b=len(b_list), pairs=tuple(pairs),
                             b_trans=tuple(t for _, _, t in b_list), n_extra=len(extras), n_side=len(side), n_out=len(out_specs),
                             acc_outs=tuple(acc_outs), row_chunk=row_chunk, epilogue=epilogue)
    outs = pl.pallas_call(
        kern,
        grid=grid,
        in_specs=in_specs + side_specs,
        out_specs=out_specs + side_specs,
        out_shape=out_shapes + [jax.ShapeDtypeStruct(w.shape, BF16) for w in side],
        compiler_params=_cparams(("parallel", "arbitrary"), vmem_mib),
        name=name,
    )(*a_list, *[b for b, _, _ in b_list], *extras, *side)
    return outs if len(outs) > 1 else outs[0]


def _ep_identity(accs, extras):
    return accs[0]


def _ep_residual(accs, extras):
    return extras[0] + accs[0]


def _ep_gated_merge(accs, extras):
    ga, gb, ya, yb = accs
    return jax.nn.sigmoid(ga) * ya + jax.nn.sigmoid(gb) * yb


def _ep_swiglu(accs, extras):
    g, u = accs
    return (g * jax.nn.sigmoid(g)) * u


N_SPLIT = 3


def _cumlog_kernel(h_ref, wf_ref, bf_ref, place_ref, *rest, n_side, ts, n_heads, inv_scale):
    side_in = rest[:n_side]
    eq_ref, ek_ref = rest[n_side:n_side + 2]
    side_out, carry_ref = rest[n_side + 2:2 * n_side + 2], rest[2 * n_side + 2]
    for src_ref, dst_ref in zip(side_in, side_out):
        dst_ref[...] = src_ref[...].astype(dst_ref.dtype)
    t = pl.program_id(1)

    @pl.when(t == 0)
    def _():
        carry_ref[...] = jnp.zeros_like(carry_ref)

    f = lax.dot_general(h_ref[...], wf_ref[...], (((1,), (1,)), ((), ())),
                        preferred_element_type=F32) + bf_ref[...]
    log_f = jnp.minimum(f, 0.0) - jnp.log(1.0 + jnp.exp(-jnp.abs(f)))
    row = lax.broadcasted_iota(jnp.int32, (ts, ts), 0)
    col = lax.broadcasted_iota(jnp.int32, (ts, ts), 1)
    tri = (col <= row).astype(F32)
    c = jnp.dot(tri, log_f, precision=lax.Precision.HIGHEST,
                preferred_element_type=F32) + carry_ref[...]
    carry_ref[...] = c[ts - 1:ts, :]
    c3 = c * inv_scale
    hi = c3.astype(BF16).astype(F32)
    mid = (c3 - hi).astype(BF16).astype(F32)
    lo = (c3 - hi - mid).astype(BF16).astype(F32)
    lane = lax.broadcasted_iota(jnp.int32, (ts, LANES), 1)
    packed = jnp.where(
        lane < n_heads, hi,
        jnp.where(lane < 2 * n_heads, pltpu.roll(mid, n_heads, 1),
                  jnp.where(lane < N_SPLIT * n_heads, pltpu.roll(lo, 2 * n_heads, 1),
                            jnp.where(lane == N_SPLIT * n_heads, 1.0, 0.0))))
    feats = jnp.dot(packed.astype(BF16), place_ref[...], preferred_element_type=F32)
    for h in range(n_heads):
        eq_ref[0, h] = feats[:, h * LANES:(h + 1) * LANES].astype(BF16)
        ek_ref[0, h] = feats[:, (n_heads + h) * LANES:(n_heads + h + 1) * LANES].astype(BF16)


def _placement(n_heads):
    p = np.zeros((LANES, 2 * n_heads * LANES), np.float32)
    one = N_SPLIT * n_heads
    for h in range(n_heads):
        for j in range(N_SPLIT):
            p[j * n_heads + h, h * LANES + j] = 1.0
            p[j * n_heads + h, (n_heads + h) * LANES + N_SPLIT + j] = -1.0
            p[one, h * LANES + N_SPLIT + j] = 1.0
            p[one, (n_heads + h) * LANES + j] = 1.0
    return jnp.asarray(p, BF16)


def _cumlog(h1, w_f, b_f, batch, seq, n_heads, dh, ts=512, side=()):
    assert (N_SPLIT * n_heads) < LANES
    d = h1.shape[1]
    nt = seq // ts
    feat = pl.BlockSpec((1, n_heads, ts, LANES), lambda b, t: (b, 0, t, 0))
    side_specs = _side_specs(side, batch * nt, lambda b, t: b * nt + t)
    return pl.pallas_call(
        functools.partial(_cumlog_kernel, n_side=len(side), ts=ts, n_heads=n_heads,
                          inv_scale=dh ** 0.5),
        grid=(batch, nt),
        in_specs=[pl.BlockSpec((ts, d), lambda b, t: (b * nt + t, 0)),
                  pl.BlockSpec((LANES, d), lambda b, t: (0, 0)),
                  pl.BlockSpec((1, LANES), lambda b, t: (0, 0)),
                  pl.BlockSpec((LANES, 2 * n_heads * LANES), lambda b, t: (0, 0))] + side_specs,
        out_specs=[feat, feat] + side_specs,
        out_shape=[jax.ShapeDtypeStruct((batch, n_heads, seq, LANES), BF16)] * 2
        + [jax.ShapeDtypeStruct(w.shape, BF16) for w in side],
        scratch_shapes=[pltpu.VMEM((1, LANES), F32)],
        compiler_params=_cparams(("parallel", "arbitrary"), 48),
        name="fox_cumlog",
    )(h1, w_f, b_f, _placement(n_heads), *side)


FOX_BQ = 512
FOX_BK = 2 * FOX_BQ


def _fox_kernel(q_ref, k_ref, v_ref, eq_ref, ek_ref, *rest, n_side, n_pairs, qk_scale):
    bq, bk = FOX_BQ, FOX_BK
    dh = q_ref.shape[2]
    side_in, o_ref = rest[:n_side], rest[n_side]
    side_out, vt_ref = rest[n_side + 1:2 * n_side + 1], rest[2 * n_side + 1]
    for src_ref, dst_ref in zip(side_in, side_out):
        dst_ref[...] = src_ref[...].astype(dst_ref.dtype)

    for c in range(vt_ref.shape[0]):
        vt_ref[c] = v_ref[0, c * bq:(c + 1) * bq, :].T

    def scores(q, c0, q0, width, masked):
        k0 = c0 * bq
        k = jnp.concatenate([k_ref[0, k0:k0 + width, :], ek_ref[0, 0, k0:k0 + width, :]], axis=1)
        s = lax.dot_general(k, q, (((1,), (1,)), ((), ())),
                            preferred_element_type=F32)
        if masked:
            key = k0 + lax.broadcasted_iota(jnp.int32, (width, bq), 0)
            qry = q0 + lax.broadcasted_iota(jnp.int32, (width, bq), 1)
            s = jnp.where(key <= qry, s, -jnp.inf)
        return s

    def fold(s, carry, c0):
        m, l, acc = carry
        width = s.shape[0]
        m_new = jnp.maximum(m, jnp.max(s, axis=0, keepdims=True))
        alpha = jnp.exp2((m - m_new) * qk_scale)
        p = jnp.exp2((s - m_new) * qk_scale)
        l = alpha * l + jnp.sum(p, axis=0, keepdims=True)
        p = p.astype(BF16)
        pv = jnp.dot(vt_ref[c0], p[:bq, :], preferred_element_type=F32)
        for t in range(1, width // bq):
            pv = pv + jnp.dot(vt_ref[c0 + t], p[t * bq:(t + 1) * bq, :],
                              preferred_element_type=F32)
        return m_new, l, alpha * acc + pv

    tiles = []
    for mi in range(n_pairs):
        tiles += [(2 * mi, 2 * mi, bq, True), (2 * mi + 1, 2 * mi, bk, True)]
        for t in range(mi):
            tiles += [(2 * mi, 2 * t, bk, False), (2 * mi + 1, 2 * t, bk, False)]
    remaining = {}
    for blk, _, _, _ in tiles:
        remaining[blk] = remaining.get(blk, 0) + 1

    q_aug, state = {}, {}

    def issue(tile):
        blk, c0, width, masked = tile
        if blk not in q_aug:
            q0 = blk * bq
            q_aug[blk] = jnp.concatenate([q_ref[0, q0:q0 + bq, :], eq_ref[0, 0, q0:q0 + bq, :]],
                                         axis=1)
            state[blk] = (jnp.full((1, bq), -jnp.inf, F32), jnp.zeros((1, bq), F32),
                          jnp.zeros((dh, bq), F32))
        return scores(q_aug[blk], c0, blk * bq, width, masked)

    def retire(tile, s):
        blk, c0, _, _ = tile
        state[blk] = fold(s, state[blk], c0)
        remaining[blk] -= 1
        if remaining[blk] == 0:
            _, l, acc = state.pop(blk)
            q_aug.pop(blk)
            o_ref[blk * bq:(blk + 1) * bq, :] = (acc / l).T.astype(o_ref.dtype)

    pending = None
    for tile in tiles:
        s = issue(tile)
        if pending is not None:
            retire(*pending)
        pending = (tile, s)
    retire(*pending)


def _fox_attention(qkv, eq, ek, batch, seq, n_heads, dh, side=()):
    assert dh == LANES and seq % FOX_BK == 0 and qkv.shape == (3 * n_heads, batch * seq, dh)
    nq = seq // FOX_BQ
    kern = functools.partial(_fox_kernel, n_side=len(side), n_pairs=nq // 2,
                             qk_scale=dh ** -0.5 * LOG2E)
    feat = pl.BlockSpec((1, 1, seq, LANES), lambda b, h: (b, h, 0, 0))
    side_specs = _side_specs(side, batch * n_heads, lambda b, h: b * n_heads + h)
    outs = pl.pallas_call(
        kern,
        grid=(batch, n_heads),
        in_specs=[pl.BlockSpec((1, seq, dh), lambda b, h: (h, b, 0)),
                  pl.BlockSpec((1, seq, dh), lambda b, h: (n_heads + h, b, 0)),
                  pl.BlockSpec((1, seq, dh), lambda b, h: (2 * n_heads + h, b, 0)),
                  feat, feat] + side_specs,
        out_specs=[pl.BlockSpec((seq, dh), lambda b, h: (b, h))] + side_specs,
        out_shape=[jax.ShapeDtypeStruct((batch * seq, n_heads * dh), BF16)]
        + [jax.ShapeDtypeStruct(w.shape, BF16) for w in side],
        scratch_shapes=[pltpu.VMEM((nq, dh, FOX_BQ), BF16)],
        compiler_params=_cparams(("parallel", "arbitrary"), 48),
        name="fox_attention",
    )(qkv, qkv, qkv, eq, ek, *side)
    return outs if side else outs[0]


def _t5_bucket(dist, num_buckets):
    max_exact = num_buckets // 2
    small = dist < max_exact
    large = max_exact + (np.log(np.maximum(dist, 1) / max_exact) / np.log(MAX_DISTANCE / max_exact)
                         * (num_buckets - max_exact)).astype(np.int64)
    large = np.minimum(large, num_buckets - 1)
    return np.where(small, dist, large)


def _bias_table_kernel(bucket_ref, rb_ref, o_ref, *, num_buckets):
    first = pl.program_id(0) == 0
    h = pl.program_id(1)
    bkt = bucket_ref[...]
    bias = jnp.zeros(bkt.shape, F32)
    for b in range(num_buckets):
        bias = jnp.where(bkt == b, rb_ref[b, h], bias)
    key = lax.broadcasted_iota(jnp.int32, bkt.shape, 0)
    qry = lax.broadcasted_iota(jnp.int32, bkt.shape, 1)
    dist = qry + BLOCK - key
    visible = (dist >= 0) & (dist < WINDOW) & ((key >= BLOCK) | jnp.logical_not(first))
    o_ref[0, 0] = jnp.where(visible, bias * LOG2E, -jnp.inf)


def _bias_table(rel_bias, n_heads):
    num_buckets = rel_bias.shape[0]
    ql = np.arange(BLOCK)[None, :]
    kl = np.arange(2 * BLOCK)[:, None]
    bucket = _t5_bucket(np.clip(ql + BLOCK - kl, 0, None), num_buckets).astype(np.int32)
    return pl.pallas_call(
        functools.partial(_bias_table_kernel, num_buckets=num_buckets),
        grid=(2, n_heads),
        in_specs=[pl.BlockSpec((2 * BLOCK, BLOCK), lambda f, h: (0, 0)),
                  pl.BlockSpec(memory_space=pltpu.SMEM)],
        out_specs=pl.BlockSpec((1, 1, 2 * BLOCK, BLOCK), lambda f, h: (f, h, 0, 0)),
        out_shape=jax.ShapeDtypeStruct((2, n_heads, 2 * BLOCK, BLOCK), F32),
        compiler_params=_cparams(("arbitrary", "arbitrary"), 32),
        name="swa_bias_table",
    )(jnp.asarray(bucket), rel_bias.astype(F32))


def _swa_kernel(q_ref, kp_ref, kc_ref, vp_ref, vc_ref, bias_ref, sink_ref, o_ref, *,
                n_kv, groups, dh, qk_scale):
    qt = q_ref[...].T
    k_band = jnp.concatenate([kp_ref[...], kc_ref[...]], axis=0)
    vt = jnp.concatenate([vp_ref[...], vc_ref[...]], axis=0).T
    outs = []
    for hk in range(n_kv):
        k = k_band[:, hk * dh:(hk + 1) * dh]
        q_grp = jnp.concatenate(
            [qt[(hk * groups + g) * dh:(hk * groups + g + 1) * dh, :] for g in range(groups)],
            axis=1)
        s = jnp.dot(k, q_grp, preferred_element_type=F32) * qk_scale
        probs, denoms = [], []
        for g in range(groups):
            hq = hk * groups + g
            sg = s[:, g * BLOCK:(g + 1) * BLOCK] + bias_ref[0, hq]
            sink = jnp.full((1, BLOCK), sink_ref[hq], F32) * LOG2E
            m = jnp.maximum(jnp.max(sg, axis=0, keepdims=True), sink)
            p = jnp.exp2(sg - m)
            denoms.append(jnp.sum(p, axis=0, keepdims=True) + jnp.exp2(sink - m))
            probs.append(p.astype(BF16))
        o = jnp.dot(vt[hk * dh:(hk + 1) * dh, :], jnp.concatenate(probs, axis=1),
                    preferred_element_type=F32)
        for g in range(groups):
            outs.append((o[:, g * BLOCK:(g + 1) * BLOCK] / denoms[g]).astype(o_ref.dtype))
    o_ref[...] = jnp.concatenate(outs, axis=0).T


def _swa_attention(qkv, bias, sinks, batch, seq, n_heads, n_kv, dh, q_col0):
    nb = seq // BLOCK
    qw = n_heads * dh
    kw = n_kv * dh
    assert q_col0 % qw == 0 and (q_col0 + qw) % kw == 0
    qb = q_col0 // qw
    kb = (q_col0 + qw) // kw
    vb = kb + 1
    prev = lambda b, i: b * nb + jnp.maximum(i - 1, 0)
    cur = lambda b, i: b * nb + i
    kern = functools.partial(_swa_kernel, n_kv=n_kv, groups=n_heads // n_kv, dh=dh,
                             qk_scale=dh ** -0.5 * LOG2E)
    return pl.pallas_call(
        kern,
        grid=(batch, nb),
        in_specs=[pl.BlockSpec((BLOCK, qw), lambda b, i: (cur(b, i), qb)),
                  pl.BlockSpec((BLOCK, kw), lambda b, i: (prev(b, i), kb)),
                  pl.BlockSpec((BLOCK, kw), lambda b, i: (cur(b, i), kb)),
                  pl.BlockSpec((BLOCK, kw), lambda b, i: (prev(b, i), vb)),
                  pl.BlockSpec((BLOCK, kw), lambda b, i: (cur(b, i), vb)),
                  pl.BlockSpec((1, n_heads, 2 * BLOCK, BLOCK),
                               lambda b, i: (jnp.minimum(i, 1), 0, 0, 0)),
                  pl.BlockSpec(memory_space=pltpu.SMEM)],
        out_specs=pl.BlockSpec((BLOCK, qw), lambda b, i: (cur(b, i), 0)),
        out_shape=jax.ShapeDtypeStruct((batch * seq, qw), BF16),
        compiler_params=_cparams(("parallel", "arbitrary"), 40),
        name="swa_attention",
    )(qkv, qkv, qkv, qkv, qkv, bias, sinks.astype(F32))


def kernel(x, norm1_g, w_in, b_forget, attn_sinks, rel_bias, w_branch_a, w_branch_b, w_out,
           norm2_g, w_ffn_gate, w_ffn_up, w_ffn_down, final_g):
    batch, seq, d = x.shape
    depth = w_in.shape[0]
    n_ha = b_forget.shape[1]
    n_hb = attn_sinks.shape[1]
    wa = w_branch_a.shape[1]
    wb = w_branch_b.shape[1]
    dh_a = wa // n_ha
    dh_b = wb // n_hb
    w_kvb = (w_in.shape[2] - 3 * wa - n_ha - wb - 2 * d) // 2
    n_kvb = w_kvb // dh_b
    c_f = 3 * wa
    c_qb = c_f + n_ha
    c_ga = c_qb + wb + 2 * w_kvb
    c_gb = c_ga + d

    xf = x.reshape(batch * seq, d)
    bias_tab = _bias_table(rel_bias, n_hb)

    for l in range(depth):
        wt = jnp.transpose(w_in[l])
        blk = np.gcd(c_qb, wt.shape[0] - c_qb)
        wt_head = _cast_rows_bf16(wt, 0, c_qb, blk, "cast_w_in_head")
        wt_tail = _cast_rows_bf16(wt, c_qb, wt.shape[0] - c_qb, blk, "cast_w_in_tail")
        w_f = jnp.pad(wt_head[c_f:], ((0, LANES - n_ha), (0, 0)))
        b_f = jnp.pad(b_forget[l].astype(F32), (0, LANES - n_ha)).reshape(1, LANES)
        w_ga = (wt_tail, c_ga - c_qb, True)
        w_gb = (wt_tail, c_gb - c_qb, True)

        h1 = _rmsnorm(xf, norm1_g[l], BF16, "rmsnorm1")
        qkv_a = _matmul([h1], [(wt_head, 0, True)], [(0, 0)], [], _ep_identity, BF16, 2048, 512,
                        n=c_f, head_dim=dh_a, vmem_mib=56, name="in_proj_fox")
        qkv_b = _matmul([h1], [(wt_tail, 0, True)], [(0, 0)], [], _ep_identity, BF16, 2048, 512,
                        n=c_ga - c_qb, vmem_mib=56, name="in_proj_swa")
        eq, ek = _cumlog(h1, w_f, b_f, batch, seq, n_ha, dh_a)
        oa, w_out_b, w_bra, w_brb = _fox_attention(
            qkv_a, eq, ek, batch, seq, n_ha, dh_a, side=(w_out[l], w_branch_a[l], w_branch_b[l]))
        ob = _swa_attention(qkv_b, bias_tab, attn_sinks[l], batch, seq, n_hb, n_kvb, dh_b, 0)
        mixed, w_gate_b, w_up_b = _matmul(
            [h1, oa, ob], [w_ga, w_gb, w_bra, w_brb],
            [(0, 0), (0, 1), (1, 2), (2, 3)], [], _ep_gated_merge, BF16, 1024, 256, n=d,
            side=(w_ffn_gate[l], w_ffn_up[l]), row_chunk=MM_ROW_CHUNK, vmem_mib=60,
            name="gated_merge")
        xf = _matmul([mixed], [w_out_b], [(0, 0)], [xf], _ep_residual, F32,
                     1024, 1024, row_chunk=MM_ROW_CHUNK, vmem_mib=58, name="out_proj")

        h2 = _rmsnorm(xf, norm2_g[l], BF16, "rmsnorm2")
        hidden, w_down_b = _matmul([h2], [w_gate_b, w_up_b], [(0, 0), (0, 1)], [], _ep_swiglu,
                                   BF16, 2048, 256, side=(w_ffn_down[l],),
                                   row_chunk=MM_ROW_CHUNK, vmem_mib=56, name="ffn_gate_up")
        xf = _matmul([hidden], [w_down_b], [(0, 0)], [xf], _ep_residual, F32,
                     512, 1024, n_outer=True, vmem_mib=60, name="ffn_down")

    out = _rmsnorm(xf, final_g, x.dtype, "rmsnorm_final")
    return out.reshape(batch, seq, d)
```

```python
import functools

import numpy as np
import jax
import jax.numpy as jnp
from jax import lax
from jax.experimental import pallas as pl
from jax.experimental.pallas import tpu as pltpu

F32 = jnp.float32
BF16 = jnp.bfloat16

EPS = 1e-6
WINDOW = 128
BLOCK = 128
MAX_DISTANCE = 128
LANES = 128
BF16_ROWS = 16
MM_ROW_CHUNK = 512
MIB = 1024 * 1024
LOG2E = 1.4426950408889634


def _cparams(semantics, vmem_mib):
    return pltpu.CompilerParams(dimension_semantics=semantics,
                                vmem_limit_bytes=vmem_mib * MIB)


def _rmsnorm_kernel(x_ref, g_ref, o_ref):
    x = x_ref[...].astype(F32)
    ms = jnp.mean(x * x, axis=-1, keepdims=True)
    o_ref[...] = (x * lax.rsqrt(ms + EPS) * g_ref[...]).astype(o_ref.dtype)


def _rmsnorm(x, g, out_dtype, name, tm=512):
    m, d = x.shape
    return pl.pallas_call(
        _rmsnorm_kernel,
        grid=(m // tm,),
        in_specs=[pl.BlockSpec((tm, d), lambda i: (i, 0)),
                  pl.BlockSpec((1, d), lambda i: (0, 0))],
        out_specs=pl.BlockSpec((tm, d), lambda i: (i, 0)),
        out_shape=jax.ShapeDtypeStruct((m, d), out_dtype),
        compiler_params=_cparams(("parallel",), 56),
        name=name,
    )(x, g.reshape(1, d).astype(F32))


def _cast_kernel(x_ref, o_ref):
    o_ref[...] = x_ref[...].astype(o_ref.dtype)


def _cast_rows_bf16(w, row0, n_rows, block_rows, name):
    c = w.shape[1]
    assert row0 % block_rows == 0 and n_rows % block_rows == 0 and block_rows % BF16_ROWS == 0
    first = row0 // block_rows
    return pl.pallas_call(
        _cast_kernel,
        grid=(n_rows // block_rows,),
        in_specs=[pl.BlockSpec((block_rows, c), lambda i: (first + i, 0))],
        out_specs=pl.BlockSpec((block_rows, c), lambda i: (i, 0)),
        out_shape=jax.ShapeDtypeStruct((n_rows, c), BF16),
        compiler_params=_cparams(("parallel",), 40),
        name=name,
    )(w)


def _mm_kernel(*refs, n_a, n_b, pairs, b_trans, n_extra, n_side, n_out, acc_outs, row_chunk,
               epilogue):
    n_in = n_a + n_b + n_extra + n_side
    a_refs = refs[:n_a]
    b_refs = refs[n_a:n_a + n_b]
    e_refs = refs[n_a + n_b:n_a + n_b + n_extra]
    side_in = refs[n_a + n_b + n_extra:n_in]
    o_refs = refs[n_in:n_in + n_out]
    side_out = refs[n_in + n_out:n_in + n_out + n_side]
    bm = o_refs[0].shape[-2]

    def extra_block(e_ref, rows):
        return e_ref[...] if e_ref.shape[0] == 1 else e_ref[rows, :]

    def finish(rows, accs):
        results = epilogue(accs, [extra_block(e, rows) for e in e_refs])
        for k, (o_ref, r) in enumerate(zip(o_refs, results if n_out > 1 else (results,))):
            if k in acc_outs:
                first = pl.program_id(1) == 0

                @pl.when(first)
                def _(o_ref=o_ref, r=r):
                    o_ref[rows, :] = r.astype(o_ref.dtype)

                @pl.when(jnp.logical_not(first))
                def _(o_ref=o_ref, r=r):
                    o_ref[rows, :] += r.astype(o_ref.dtype)
            elif len(o_ref.shape) == 3:
                dh = o_ref.shape[2]
                for t in range(o_ref.shape[0]):
                    o_ref[t, rows, :] = r[:, t * dh:(t + 1) * dh].astype(o_ref.dtype)
            else:
                o_ref[rows, :] = r.astype(o_ref.dtype)

    pending = None
    for r in range(bm // row_chunk):
        rows = pl.ds(r * row_chunk, row_chunk)
        accs = [lax.dot_general(a_refs[ai][rows, :], b_refs[bi][...],
                                (((1,), (1 if b_trans[bi] else 0,)), ((), ())),
                                preferred_element_type=F32)
                for ai, bi in pairs]
        if pending is not None:
            finish(*pending)
        pending = (rows, accs)
    finish(*pending)
    for src_ref, dst_ref in zip(side_in, side_out):
        dst_ref[...] = src_ref[...].astype(dst_ref.dtype)


def _side_window(w):
    return w if isinstance(w, tuple) else (w, 0, w.shape[0])


def _side_specs(side, n_steps, step_index):
    in_specs, out_specs, out_shapes = [], [], []
    for w, row0, n_rows in map(_side_window, side):
        rows = n_rows // n_steps
        assert rows * n_steps == n_rows and rows % BF16_ROWS == 0 and row0 % rows == 0
        in_specs.append(pl.BlockSpec((rows, w.shape[1]),
                                     lambda *g, first=row0 // rows: (first + step_index(*g), 0)))
        out_specs.append(pl.BlockSpec((rows, w.shape[1]), lambda *g: (step_index(*g), 0)))
        out_shapes.append(jax.ShapeDtypeStruct((n_rows, w.shape[1]), BF16))
    return in_specs, out_specs, out_shapes


def _matmul(a_list, b_list, pairs, extras, epilogue, out_dtype, bm, bn, *,
            n=None, n_outer=False, side=(), more_outs=(), row_chunk=None, head_dim=None,
            vmem_mib=48, name):
    b_list = [(tuple(b) + (False,))[:3] if isinstance(b, tuple) else (b, 0, False) for b in b_list]
    row_chunk = bm if row_chunk is None else row_chunk
    assert bm % row_chunk == 0
    m = a_list[0].shape[0]
    if n is None:
        b0, off0, t0 = b_list[0]
        n = b0.shape[0 if t0 else 1] - off0
    assert m % bm == 0 and n % bn == 0
    if n_outer:
        grid = (n // bn, m // bm)
        mi = lambda j, i: i
        nj = lambda j, i: j
    else:
        grid = (m // bm, n // bn)
        mi = lambda i, j: i
        nj = lambda i, j: j
    in_specs = []
    for a in a_list:
        assert a.shape[0] == m
        in_specs.append(pl.BlockSpec((bm, a.shape[1]), lambda *g: (mi(*g), 0)))
    for b, off, trans in b_list:
        assert off % bn == 0 and b.shape[0 if trans else 1] >= off + n
        if trans:
            assert not n_outer
            in_specs.append(pl.BlockSpec((bn, b.shape[1]),
                                         lambda *g, blk=off // bn: (blk + nj(*g), 0)))
            continue
        index = lambda *g, blk=off // bn: (0, blk + nj(*g))
        if n_outer:
            in_specs.append(pl.BlockSpec((b.shape[0], bn), index, pipeline_mode=pl.Buffered(1)))
        else:
            in_specs.append(pl.BlockSpec((b.shape[0], bn), index))
    tile_spec = pl.BlockSpec((bm, bn), lambda *g: (mi(*g), nj(*g)))
    for e in extras:
        if e.shape[0] == 1:
            in_specs.append(pl.BlockSpec((1, bn), lambda *g: (0, nj(*g))))
        elif e.shape == (m, n):
            in_specs.append(tile_spec)
        else:
            assert e.shape[0] == m
            in_specs.append(pl.BlockSpec((bm, e.shape[1]), lambda *g: (mi(*g), 0)))
    out_specs, out_shapes, acc_outs = [tile_spec], [jax.ShapeDtypeStruct((m, n), out_dtype)], []
    if head_dim is not None:
        assert bn % head_dim == 0 and not more_outs
        out_specs = [pl.BlockSpec((bn // head_dim, bm, head_dim), lambda *g: (nj(*g), mi(*g), 0))]
        out_shapes = [jax.ShapeDtypeStruct((n // head_dim, m, head_dim), out_dtype)]
    for dtype, kind in more_outs:
        if kind == "tile":
            out_specs.append(tile_spec)
            out_shapes.append(jax.ShapeDtypeStruct((m, n), dtype))
        else:
            assert kind == "rowacc" and not n_outer
            acc_outs.append(len(out_specs))
            out_specs.append(pl.BlockSpec((bm, LANES), lambda *g: (mi(*g), 0)))
            out_shapes.append(jax.ShapeDtypeStruct((m, LANES), dtype))
    side_in, side_out, side_shapes = _side_specs(side, grid[0] * grid[1],
                                                 lambda g0, g1: g0 * grid[1] + g1)
    kern = functools.partial(_mm_kernel, n_a=len(a_list), n_b=len(b_list), pairs=tuple(pairs),
                             b_trans=tuple(t for _, _, t in b_list), n_extra=len(extras), n_side=len(side), n_out=len(out_specs),
                             acc_outs=tuple(acc_outs), row_chunk=row_chunk, epilogue=epilogue)
    outs = pl.pallas_call(
        kern,
        grid=grid,
        in_specs=in_specs + side_in,
        out_specs=out_specs + side_out,
        out_shape=out_shapes + side_shapes,
        compiler_params=_cparams(("parallel", "arbitrary"), vmem_mib),
        name=name,
    )(*a_list, *[b for b, _, _ in b_list], *extras, *[_side_window(w)[0] for w in side])
    return outs if len(outs) > 1 else outs[0]


def _ep_identity(accs, extras):
    return accs[0]


def _ep_residual(accs, extras):
    return extras[0] + accs[0]


def _ep_gated_merge(accs, extras):
    ga, gb, ya, yb = accs
    return jax.nn.sigmoid(ga) * ya + jax.nn.sigmoid(gb) * yb


def _ep_swiglu(accs, extras):
    g, u = accs
    return (g * jax.nn.sigmoid(g)) * u


N_SPLIT = 3


def _cumlog_kernel(h_ref, wf_ref, bf_ref, place_ref, *rest, n_side, ts, n_heads, inv_scale):
    side_in = rest[:n_side]
    eq_ref, ek_ref = rest[n_side:n_side + 2]
    side_out, carry_ref = rest[n_side + 2:2 * n_side + 2], rest[2 * n_side + 2]
    for src_ref, dst_ref in zip(side_in, side_out):
        dst_ref[...] = src_ref[...].astype(dst_ref.dtype)
    t = pl.program_id(1)

    @pl.when(t == 0)
    def _():
        carry_ref[...] = jnp.zeros_like(carry_ref)

    nt_dims = (((1,), (1,)), ((), ()))
    half = h_ref.shape[1] // 2
    f = (lax.dot_general(h_ref[:, :half], wf_ref[:, :half], nt_dims, preferred_element_type=F32)
         + lax.dot_general(h_ref[:, half:], wf_ref[:, half:], nt_dims, preferred_element_type=F32)
         + bf_ref[...])
    log_f = jnp.minimum(f, 0.0) - jnp.log(1.0 + jnp.exp(-jnp.abs(f)))
    row = lax.broadcasted_iota(jnp.int32, (ts, ts), 0)
    col = lax.broadcasted_iota(jnp.int32, (ts, ts), 1)
    tri = (col <= row).astype(F32)
    c = jnp.concatenate(
        [jnp.dot(tri[r:r + ts // 2], log_f, precision=lax.Precision.HIGHEST,
                 preferred_element_type=F32) for r in (0, ts // 2)], axis=0) + carry_ref[...]
    carry_ref[...] = c[ts - 1:ts, :]
    c3 = c * inv_scale
    hi = c3.astype(BF16).astype(F32)
    mid = (c3 - hi).astype(BF16).astype(F32)
    lo = (c3 - hi - mid).astype(BF16).astype(F32)
    lane = lax.broadcasted_iota(jnp.int32, (ts, LANES), 1)
    packed = jnp.where(
        lane < n_heads, hi,
        jnp.where(lane < 2 * n_heads, pltpu.roll(mid, n_heads, 1),
                  jnp.where(lane < N_SPLIT * n_heads, pltpu.roll(lo, 2 * n_heads, 1),
                            jnp.where(lane == N_SPLIT * n_heads, 1.0, 0.0))))
    feats = jnp.dot(packed.astype(BF16), place_ref[...], preferred_element_type=F32)
    for h in range(n_heads):
        eq_ref[0, h] = feats[:, h * LANES:(h + 1) * LANES].astype(BF16)
        ek_ref[0, h] = feats[:, (n_heads + h) * LANES:(n_heads + h + 1) * LANES].astype(BF16)


def _placement(n_heads):
    p = np.zeros((LANES, 2 * n_heads * LANES), np.float32)
    one = N_SPLIT * n_heads
    for h in range(n_heads):
        for j in range(N_SPLIT):
            p[j * n_heads + h, h * LANES + j] = 1.0
            p[j * n_heads + h, (n_heads + h) * LANES + N_SPLIT + j] = -1.0
            p[one, h * LANES + N_SPLIT + j] = 1.0
            p[one, (n_heads + h) * LANES + j] = 1.0
    return jnp.asarray(p, BF16)


def _cumlog(h1, w_f, b_f, batch, seq, n_heads, dh, ts=512, side=()):
    assert (N_SPLIT * n_heads) < LANES
    d = h1.shape[1]
    nt = seq // ts
    feat = pl.BlockSpec((1, n_heads, ts, LANES), lambda b, t: (b, 0, t, 0))
    side_in, side_out, side_shapes = _side_specs(side, batch * nt, lambda b, t: b * nt + t)
    return pl.pallas_call(
        functools.partial(_cumlog_kernel, n_side=len(side), ts=ts, n_heads=n_heads,
                          inv_scale=dh ** 0.5),
        grid=(batch, nt),
        in_specs=[pl.BlockSpec((ts, d), lambda b, t: (b * nt + t, 0)),
                  pl.BlockSpec((LANES, d), lambda b, t: (0, 0)),
                  pl.BlockSpec((1, LANES), lambda b, t: (0, 0)),
                  pl.BlockSpec((LANES, 2 * n_heads * LANES), lambda b, t: (0, 0))] + side_in,
        out_specs=[feat, feat] + side_out,
        out_shape=[jax.ShapeDtypeStruct((batch, n_heads, seq, LANES), BF16)] * 2 + side_shapes,
        scratch_shapes=[pltpu.VMEM((1, LANES), F32)],
        compiler_params=_cparams(("parallel", "arbitrary"), 48),
        name="fox_cumlog",
    )(h1, w_f, b_f, _placement(n_heads), *[_side_window(w)[0] for w in side])


FOX_BQ = 512
FOX_BK = 2 * FOX_BQ


def _fox_kernel(q_ref, k_ref, v_ref, eq_ref, ek_ref, *rest, n_side, n_pairs, qk_scale):
    bq, bk = FOX_BQ, FOX_BK
    dh = q_ref.shape[2]
    side_in, o_ref = rest[:n_side], rest[n_side]
    side_out, vt_ref = rest[n_side + 1:2 * n_side + 1], rest[2 * n_side + 1]
    for src_ref, dst_ref in zip(side_in, side_out):
        dst_ref[...] = src_ref[...].astype(dst_ref.dtype)

    for c in range(vt_ref.shape[0]):
        vt_ref[c] = v_ref[0, c * bq:(c + 1) * bq, :].T

    def scores(q, c0, q0, width, masked):
        k0 = c0 * bq
        k = jnp.concatenate([k_ref[0, k0:k0 + width, :], ek_ref[0, 0, k0:k0 + width, :]], axis=1)
        s = lax.dot_general(k, q, (((1,), (1,)), ((), ())),
                            preferred_element_type=F32)
        if masked:
            key = k0 + lax.broadcasted_iota(jnp.int32, (width, bq), 0)
            qry = q0 + lax.broadcasted_iota(jnp.int32, (width, bq), 1)
            s = jnp.where(key <= qry, s, -jnp.inf)
        return s

    def fold(s, carry, c0):
        m, l, acc = carry
        width = s.shape[0]
        m_new = jnp.maximum(m, jnp.max(s, axis=0, keepdims=True))
        alpha = jnp.exp2((m - m_new) * qk_scale)
        p = jnp.exp2((s - m_new) * qk_scale)
        l = alpha * l + jnp.sum(p, axis=0, keepdims=True)
        p = p.astype(BF16)
        pv = jnp.dot(vt_ref[c0], p[:bq, :], preferred_element_type=F32)
        for t in range(1, width // bq):
            pv = pv + jnp.dot(vt_ref[c0 + t], p[t * bq:(t + 1) * bq, :],
                              preferred_element_type=F32)
        return m_new, l, alpha * acc + pv

    tiles = []
    for mi in range(n_pairs):
        tiles += [(2 * mi, 2 * mi, bq, True), (2 * mi + 1, 2 * mi, bk, True)]
        for t in range(mi):
            tiles += [(2 * mi, 2 * t, bk, False), (2 * mi + 1, 2 * t, bk, False)]
    remaining = {}
    for blk, _, _, _ in tiles:
        remaining[blk] = remaining.get(blk, 0) + 1

    q_aug, state = {}, {}

    def issue(tile):
        blk, c0, width, masked = tile
        if blk not in q_aug:
            q0 = blk * bq
            q_aug[blk] = jnp.concatenate([q_ref[0, q0:q0 + bq, :], eq_ref[0, 0, q0:q0 + bq, :]],
                                         axis=1)
            state[blk] = (jnp.full((1, bq), -jnp.inf, F32), jnp.zeros((1, bq), F32),
                          jnp.zeros((dh, bq), F32))
        return scores(q_aug[blk], c0, blk * bq, width, masked)

    def retire(tile, s):
        blk, c0, _, _ = tile
        state[blk] = fold(s, state[blk], c0)
        remaining[blk] -= 1
        if remaining[blk] == 0:
            _, l, acc = state.pop(blk)
            q_aug.pop(blk)
            o_ref[blk * bq:(blk + 1) * bq, :] = (acc / l).T.astype(o_ref.dtype)

    pending = None
    for tile in tiles:
        s = issue(tile)
        if pending is not None:
            retire(*pending)
        pending = (tile, s)
    retire(*pending)


def _fox_attention(qkv, eq, ek, batch, seq, n_heads, dh, side=()):
    assert dh == LANES and seq % FOX_BK == 0 and qkv.shape == (3 * n_heads, batch * seq, dh)
    nq = seq // FOX_BQ
    kern = functools.partial(_fox_kernel, n_side=len(side), n_pairs=nq // 2,
                             qk_scale=dh ** -0.5 * LOG2E)
    feat = pl.BlockSpec((1, 1, seq, LANES), lambda b, h: (b, h, 0, 0))
    side_in, side_out, side_shapes = _side_specs(side, batch * n_heads,
                                                 lambda b, h: b * n_heads + h)
    outs = pl.pallas_call(
        kern,
        grid=(batch, n_heads),
        in_specs=[pl.BlockSpec((1, seq, dh), lambda b, h: (h, b, 0)),
                  pl.BlockSpec((1, seq, dh), lambda b, h: (n_heads + h, b, 0)),
                  pl.BlockSpec((1, seq, dh), lambda b, h: (2 * n_heads + h, b, 0)),
                  feat, feat] + side_in,
        out_specs=[pl.BlockSpec((seq, dh), lambda b, h: (b, h))] + side_out,
        out_shape=[jax.ShapeDtypeStruct((batch * seq, n_heads * dh), BF16)] + side_shapes,
        scratch_shapes=[pltpu.VMEM((nq, dh, FOX_BQ), BF16)],
        compiler_params=_cparams(("parallel", "arbitrary"), 48),
        name="fox_attention",
    )(qkv, qkv, qkv, eq, ek, *[_side_window(w)[0] for w in side])
    return outs if side else outs[0]


def _t5_bucket(dist, num_buckets):
    max_exact = num_buckets // 2
    small = dist < max_exact
    large = max_exact + (np.log(np.maximum(dist, 1) / max_exact) / np.log(MAX_DISTANCE / max_exact)
                         * (num_buckets - max_exact)).astype(np.int64)
    large = np.minimum(large, num_buckets - 1)
    return np.where(small, dist, large)


def _bias_table_kernel(bucket_ref, rb_ref, o_ref, *, num_buckets):
    first = pl.program_id(0) == 0
    h = pl.program_id(1)
    bkt = bucket_ref[...]
    bias = jnp.zeros(bkt.shape, F32)
    for b in range(num_buckets):
        bias = jnp.where(bkt == b, rb_ref[b, h], bias)
    key = lax.broadcasted_iota(jnp.int32, bkt.shape, 0)
    qry = lax.broadcasted_iota(jnp.int32, bkt.shape, 1)
    dist = qry + BLOCK - key
    visible = (dist >= 0) & (dist < WINDOW) & ((key >= BLOCK) | jnp.logical_not(first))
    o_ref[0, 0] = jnp.where(visible, bias * LOG2E, -jnp.inf)


def _bias_table(rel_bias, n_heads):
    num_buckets = rel_bias.shape[0]
    ql = np.arange(BLOCK)[None, :]
    kl = np.arange(2 * BLOCK)[:, None]
    bucket = _t5_bucket(np.clip(ql + BLOCK - kl, 0, None), num_buckets).astype(np.int32)
    return pl.pallas_call(
        functools.partial(_bias_table_kernel, num_buckets=num_buckets),
        grid=(2, n_heads),
        in_specs=[pl.BlockSpec((2 * BLOCK, BLOCK), lambda f, h: (0, 0)),
                  pl.BlockSpec(memory_space=pltpu.SMEM)],
        out_specs=pl.BlockSpec((1, 1, 2 * BLOCK, BLOCK), lambda f, h: (f, h, 0, 0)),
        out_shape=jax.ShapeDtypeStruct((2, n_heads, 2 * BLOCK, BLOCK), F32),
        compiler_params=_cparams(("arbitrary", "arbitrary"), 32),
        name="swa_bias_table",
    )(jnp.asarray(bucket), rel_bias.astype(F32))


def _swa_kernel(q_ref, kp_ref, kc_ref, vp_ref, vc_ref, bias_ref, sink_ref, o_ref, *,
                n_kv, groups, dh, qk_scale):
    qt = q_ref[...].T
    k_band = jnp.concatenate([kp_ref[...], kc_ref[...]], axis=0)
    vt = jnp.concatenate([vp_ref[...], vc_ref[...]], axis=0).T
    outs = []
    for hk in range(n_kv):
        k = k_band[:, hk * dh:(hk + 1) * dh]
        q_grp = jnp.concatenate(
            [qt[(hk * groups + g) * dh:(hk * groups + g + 1) * dh, :] for g in range(groups)],
            axis=1)
        s = jnp.dot(k, q_grp, preferred_element_type=F32) * qk_scale
        probs, denoms = [], []
        for g in range(groups):
            hq = hk * groups + g
            sg = s[:, g * BLOCK:(g + 1) * BLOCK] + bias_ref[0, hq]
            sink = jnp.full((1, BLOCK), sink_ref[hq], F32) * LOG2E
            m = jnp.maximum(jnp.max(sg, axis=0, keepdims=True), sink)
            p = jnp.exp2(sg - m)
            denoms.append(jnp.sum(p, axis=0, keepdims=True) + jnp.exp2(sink - m))
            probs.append(p.astype(BF16))
        o = jnp.dot(vt[hk * dh:(hk + 1) * dh, :], jnp.concatenate(probs, axis=1),
                    preferred_element_type=F32)
        for g in range(groups):
            outs.append((o[:, g * BLOCK:(g + 1) * BLOCK] / denoms[g]).astype(o_ref.dtype))
    o_ref[...] = jnp.concatenate(outs, axis=0).T


def _swa_attention(qkv, bias, sinks, batch, seq, n_heads, n_kv, dh, q_col0):
    nb = seq // BLOCK
    qw = n_heads * dh
    kw = n_kv * dh
    assert q_col0 % qw == 0 and (q_col0 + qw) % kw == 0
    qb = q_col0 // qw
    kb = (q_col0 + qw) // kw
    vb = kb + 1
    prev = lambda b, i: b * nb + jnp.maximum(i - 1, 0)
    cur = lambda b, i: b * nb + i
    kern = functools.partial(_swa_kernel, n_kv=n_kv, groups=n_heads // n_kv, dh=dh,
                             qk_scale=dh ** -0.5 * LOG2E)
    return pl.pallas_call(
        kern,
        grid=(batch, nb),
        in_specs=[pl.BlockSpec((BLOCK, qw), lambda b, i: (cur(b, i), qb)),
                  pl.BlockSpec((BLOCK, kw), lambda b, i: (prev(b, i), kb)),
                  pl.BlockSpec((BLOCK, kw), lambda b, i: (cur(b, i), kb)),
                  pl.BlockSpec((BLOCK, kw), lambda b, i: (prev(b, i), vb)),
                  pl.BlockSpec((BLOCK, kw), lambda b, i: (cur(b, i), vb)),
                  pl.BlockSpec((1, n_heads, 2 * BLOCK, BLOCK),
                               lambda b, i: (jnp.minimum(i, 1), 0, 0, 0)),
                  pl.BlockSpec(memory_space=pltpu.SMEM)],
        out_specs=pl.BlockSpec((BLOCK, qw), lambda b, i: (cur(b, i), 0)),
        out_shape=jax.ShapeDtypeStruct((batch * seq, qw), BF16),
        compiler_params=_cparams(("parallel", "arbitrary"), 40),
        name="swa_attention",
    )(qkv, qkv, qkv, qkv, qkv, bias, sinks.astype(F32))


def kernel(x, norm1_g, w_in, b_forget, attn_sinks, rel_bias, w_branch_a, w_branch_b, w_out,
           norm2_g, w_ffn_gate, w_ffn_up, w_ffn_down, final_g):
    batch, seq, d = x.shape
    depth = w_in.shape[0]
    n_ha = b_forget.shape[1]
    n_hb = attn_sinks.shape[1]
    wa = w_branch_a.shape[1]
    wb = w_branch_b.shape[1]
    dh_a = wa // n_ha
    dh_b = wb // n_hb
    w_kvb = (w_in.shape[2] - 3 * wa - n_ha - wb - 2 * d) // 2
    n_kvb = w_kvb // dh_b
    c_f = 3 * wa
    c_qb = c_f + n_ha
    c_ga = c_qb + wb + 2 * w_kvb
    c_gb = c_ga + d

    xf = x.reshape(batch * seq, d)
    bias_tab = _bias_table(rel_bias, n_hb)

    for l in range(depth):
        wt = jnp.transpose(w_in[l])
        blk = np.gcd(c_qb, wt.shape[0] - c_qb)
        wt_head = _cast_rows_bf16(wt, 0, c_qb, blk, "cast_w_in_head")
        w_f = jnp.pad(wt_head[c_f:], ((0, LANES - n_ha), (0, 0)))
        b_f = jnp.pad(b_forget[l].astype(F32), (0, LANES - n_ha)).reshape(1, LANES)

        h1 = _rmsnorm(xf, norm1_g[l], BF16, "rmsnorm1")
        qkv_a, wt_tail = _matmul(
            [h1], [(wt_head, 0, True)], [(0, 0)], [], _ep_identity, BF16, 2048, 512, n=c_f,
            head_dim=dh_a, side=((wt, c_qb, wt.shape[0] - c_qb),), vmem_mib=58, name="in_proj_fox")
        w_ga = (wt_tail, c_ga - c_qb, True)
        w_gb = (wt_tail, c_gb - c_qb, True)
        qkv_b = _matmul([h1], [(wt_tail, 0, True)], [(0, 0)], [], _ep_identity, BF16, 2048, 512,
                        n=c_ga - c_qb, vmem_mib=56, name="in_proj_swa")
        eq, ek = _cumlog(h1, w_f, b_f, batch, seq, n_ha, dh_a)
        oa, w_out_b, w_bra, w_brb = _fox_attention(
            qkv_a, eq, ek, batch, seq, n_ha, dh_a, side=(w_out[l], w_branch_a[l], w_branch_b[l]))
        ob = _swa_attention(qkv_b, bias_tab, attn_sinks[l], batch, seq, n_hb, n_kvb, dh_b, 0)
        mixed, w_gate_b, w_up_b = _matmul(
            [h1, oa, ob], [w_ga, w_gb, w_bra, w_brb],
            [(0, 0), (0, 1), (1, 2), (2, 3)], [], _ep_gated_merge, BF16, 1024, 256, n=d,
            side=(w_ffn_gate[l], w_ffn_up[l]), row_chunk=MM_ROW_CHUNK, vmem_mib=60,
            name="gated_merge")
        xf = _matmul([mixed], [w_out_b], [(0, 0)], [xf], _ep_residual, F32,
                     1024, 1024, row_chunk=MM_ROW_CHUNK, vmem_mib=58, name="out_proj")

        h2 = _rmsnorm(xf, norm2_g[l], BF16, "rmsnorm2")
        hidden, w_down_b = _matmul([h2], [w_gate_b, w_up_b], [(0, 0), (0, 1)], [], _ep_swiglu,
                                   BF16, 2048, 256, side=(w_ffn_down[l],),
                                   row_chunk=MM_ROW_CHUNK, vmem_mib=56, name="ffn_gate_up")
        xf = _matmul([hidden], [w_down_b], [(0, 0)], [xf], _ep_residual, F32,
                     512, 1024, n_outer=True, vmem_mib=60, name="ffn_down")

    out = _rmsnorm(xf, final_g, x.dtype, "rmsnorm_final")
    return out.reshape(batch, seq, d)
```

```python
import functools

import numpy as np
import jax
import jax.numpy as jnp
from jax import lax
from jax.experimental import pallas as pl
from jax.experimental.pallas import tpu as pltpu

F32 = jnp.float32
BF16 = jnp.bfloat16

EPS = 1e-6
WINDOW = 128
BLOCK = 128
MAX_DISTANCE = 128
LANES = 128
BF16_ROWS = 16
MM_ROW_CHUNK = 512
CAST_MAX_ROWS = 640
MIB = 1024 * 1024
LOG2E = 1.4426950408889634


def _cparams(semantics, vmem_mib):
    return pltpu.CompilerParams(dimension_semantics=semantics,
                                vmem_limit_bytes=vmem_mib * MIB)


def _rmsnorm_kernel(x_ref, g_ref, o_ref):
    x = x_ref[...].astype(F32)
    ms = jnp.mean(x * x, axis=-1, keepdims=True)
    o_ref[...] = (x * lax.rsqrt(ms + EPS) * g_ref[...]).astype(o_ref.dtype)


def _rmsnorm(x, g, out_dtype, name, tm=512):
    m, d = x.shape
    return pl.pallas_call(
        _rmsnorm_kernel,
        grid=(m // tm,),
        in_specs=[pl.BlockSpec((tm, d), lambda i: (i, 0)),
                  pl.BlockSpec((1, d), lambda i: (0, 0))],
        out_specs=pl.BlockSpec((tm, d), lambda i: (i, 0)),
        out_shape=jax.ShapeDtypeStruct((m, d), out_dtype),
        compiler_params=_cparams(("parallel",), 56),
        name=name,
    )(x, g.reshape(1, d).astype(F32))


def _cast_kernel(x_ref, o_ref):
    o_ref[...] = x_ref[...].astype(o_ref.dtype)


def _cast_rows_bf16(w, row0, n_rows, block_rows, name):
    c = w.shape[1]
    assert row0 % block_rows == 0 and n_rows % block_rows == 0 and block_rows % BF16_ROWS == 0
    first = row0 // block_rows
    return pl.pallas_call(
        _cast_kernel,
        grid=(n_rows // block_rows,),
        in_specs=[pl.BlockSpec((block_rows, c), lambda i: (first + i, 0))],
        out_specs=pl.BlockSpec((block_rows, c), lambda i: (i, 0)),
        out_shape=jax.ShapeDtypeStruct((n_rows, c), BF16),
        compiler_params=_cparams(("parallel",), 40),
        name=name,
    )(w)


def _mm_kernel(*refs, n_a, n_b, pairs, b_trans, n_extra, n_side, n_out, acc_outs, row_chunk,
               epilogue):
    n_in = n_a + n_b + n_extra + n_side
    a_refs = refs[:n_a]
    b_refs = refs[n_a:n_a + n_b]
    e_refs = refs[n_a + n_b:n_a + n_b + n_extra]
    side_in = refs[n_a + n_b + n_extra:n_in]
    o_refs = refs[n_in:n_in + n_out]
    side_out = refs[n_in + n_out:n_in + n_out + n_side]
    bm = o_refs[0].shape[-2]

    def extra_block(e_ref, rows):
        return e_ref[...] if e_ref.shape[0] == 1 else e_ref[rows, :]

    def finish(rows, accs):
        results = epilogue(accs, [extra_block(e, rows) for e in e_refs])
        for k, (o_ref, r) in enumerate(zip(o_refs, results if n_out > 1 else (results,))):
            if k in acc_outs:
                first = pl.program_id(1) == 0

                @pl.when(first)
                def _(o_ref=o_ref, r=r):
                    o_ref[rows, :] = r.astype(o_ref.dtype)

                @pl.when(jnp.logical_not(first))
                def _(o_ref=o_ref, r=r):
                    o_ref[rows, :] += r.astype(o_ref.dtype)
            elif len(o_ref.shape) == 3:
                dh = o_ref.shape[2]
                for t in range(o_ref.shape[0]):
                    o_ref[t, rows, :] = r[:, t * dh:(t + 1) * dh].astype(o_ref.dtype)
            else:
                o_ref[rows, :] = r.astype(o_ref.dtype)

    pending = None
    for r in range(bm // row_chunk):
        rows = pl.ds(r * row_chunk, row_chunk)
        accs = [lax.dot_general(a_refs[ai][rows, :], b_refs[bi][...],
                                (((1,), (1 if b_trans[bi] else 0,)), ((), ())),
                                preferred_element_type=F32)
                for ai, bi in pairs]
        if pending is not None:
            finish(*pending)
        pending = (rows, accs)
    finish(*pending)
    for src_ref, dst_ref in zip(side_in, side_out):
        dst_ref[...] = src_ref[...].astype(dst_ref.dtype)


def _side_window(w):
    return w if isinstance(w, tuple) else (w, 0, w.shape[0])


def _side_specs(side, n_steps, step_index):
    in_specs, out_specs, out_shapes = [], [], []
    for w, row0, n_rows in map(_side_window, side):
        rows = n_rows // n_steps
        assert rows * n_steps == n_rows and rows % BF16_ROWS == 0 and row0 % rows == 0
        in_specs.append(pl.BlockSpec((rows, w.shape[1]),
                                     lambda *g, first=row0 // rows: (first + step_index(*g), 0)))
        out_specs.append(pl.BlockSpec((rows, w.shape[1]), lambda *g: (step_index(*g), 0)))
        out_shapes.append(jax.ShapeDtypeStruct((n_rows, w.shape[1]), BF16))
    return in_specs, out_specs, out_shapes


def _matmul(a_list, b_list, pairs, extras, epilogue, out_dtype, bm, bn, *,
            n=None, n_outer=False, side=(), more_outs=(), row_chunk=None, head_dim=None,
            vmem_mib=48, name):
    b_list = [(tuple(b) + (False,))[:3] if isinstance(b, tuple) else (b, 0, False) for b in b_list]
    row_chunk = bm if row_chunk is None else row_chunk
    assert bm % row_chunk == 0
    m = a_list[0].shape[0]
    if n is None:
        b0, off0, t0 = b_list[0]
        n = b0.shape[0 if t0 else 1] - off0
    assert m % bm == 0 and n % bn == 0
    if n_outer:
        grid = (n // bn, m // bm)
        mi = lambda j, i: i
        nj = lambda j, i: j
    else:
        grid = (m // bm, n // bn)
        mi = lambda i, j: i
        nj = lambda i, j: j
    in_specs = []
    for a in a_list:
        assert a.shape[0] == m
        in_specs.append(pl.BlockSpec((bm, a.shape[1]), lambda *g: (mi(*g), 0)))
    for b, off, trans in b_list:
        assert off % bn == 0 and b.shape[0 if trans else 1] >= off + n
        if trans:
            assert not n_outer
            in_specs.append(pl.BlockSpec((bn, b.shape[1]),
                                         lambda *g, blk=off // bn: (blk + nj(*g), 0)))
            continue
        index = lambda *g, blk=off // bn: (0, blk + nj(*g))
        if n_outer:
            in_specs.append(pl.BlockSpec((b.shape[0], bn), index, pipeline_mode=pl.Buffered(1)))
        else:
            in_specs.append(pl.BlockSpec((b.shape[0], bn), index))
    tile_spec = pl.BlockSpec((bm, bn), lambda *g: (mi(*g), nj(*g)))
    for e in extras:
        if e.shape[0] == 1:
            in_specs.append(pl.BlockSpec((1, bn), lambda *g: (0, nj(*g))))
        elif e.shape == (m, n):
            in_specs.append(tile_spec)
        else:
            assert e.shape[0] == m
            in_specs.append(pl.BlockSpec((bm, e.shape[1]), lambda *g: (mi(*g), 0)))
    out_specs, out_shapes, acc_outs = [tile_spec], [jax.ShapeDtypeStruct((m, n), out_dtype)], []
    if head_dim is not None:
        assert bn % head_dim == 0 and not more_outs
        out_specs = [pl.BlockSpec((bn // head_dim, bm, head_dim), lambda *g: (nj(*g), mi(*g), 0))]
        out_shapes = [jax.ShapeDtypeStruct((n // head_dim, m, head_dim), out_dtype)]
    for dtype, kind in more_outs:
        if kind == "tile":
            out_specs.append(tile_spec)
            out_shapes.append(jax.ShapeDtypeStruct((m, n), dtype))
        else:
            assert kind == "rowacc" and not n_outer
            acc_outs.append(len(out_specs))
            out_specs.append(pl.BlockSpec((bm, LANES), lambda *g: (mi(*g), 0)))
            out_shapes.append(jax.ShapeDtypeStruct((m, LANES), dtype))
    side_in, side_out, side_shapes = _side_specs(side, grid[0] * grid[1],
                                                 lambda g0, g1: g0 * grid[1] + g1)
    kern = functools.partial(_mm_kernel, n_a=len(a_list), n_b=len(b_list), pairs=tuple(pairs),
                             b_trans=tuple(t for _, _, t in b_list), n_extra=len(extras), n_side=len(side), n_out=len(out_specs),
                             acc_outs=tuple(acc_outs), row_chunk=row_chunk, epilogue=epilogue)
    outs = pl.pallas_call(
        kern,
        grid=grid,
        in_specs=in_specs + side_in,
        out_specs=out_specs + side_out,
        out_shape=out_shapes + side_shapes,
        compiler_params=_cparams(("parallel", "arbitrary"), vmem_mib),
        name=name,
    )(*a_list, *[b for b, _, _ in b_list], *extras, *[_side_window(w)[0] for w in side])
    return outs if len(outs) > 1 else outs[0]


def _ep_identity(accs, extras):
    return accs[0]


def _ep_residual(accs, extras):
    return extras[0] + accs[0]


def _ep_gated_merge(accs, extras):
    ga, gb, ya, yb = accs
    return jax.nn.sigmoid(ga) * ya + jax.nn.sigmoid(gb) * yb


def _ep_swiglu(accs, extras):
    g, u = accs
    return (g * jax.nn.sigmoid(g)) * u


N_SPLIT = 3


def _cumlog_kernel(h_ref, wf_ref, bf_ref, place_ref, *rest, n_side, ts, n_heads, inv_scale):
    side_in = rest[:n_side]
    eq_ref, ek_ref = rest[n_side:n_side + 2]
    side_out, carry_ref = rest[n_side + 2:2 * n_side + 2], rest[2 * n_side + 2]
    for src_ref, dst_ref in zip(side_in, side_out):
        dst_ref[...] = src_ref[...].astype(dst_ref.dtype)
    t = pl.program_id(1)

    @pl.when(t == 0)
    def _():
        carry_ref[...] = jnp.zeros_like(carry_ref)

    nt_dims = (((1,), (1,)), ((), ()))
    half = h_ref.shape[1] // 2
    f = (lax.dot_general(h_ref[:, :half], wf_ref[:, :half], nt_dims, preferred_element_type=F32)
         + lax.dot_general(h_ref[:, half:], wf_ref[:, half:], nt_dims, preferred_element_type=F32)
         + bf_ref[...])
    log_f = jnp.minimum(f, 0.0) - jnp.log(1.0 + jnp.exp(-jnp.abs(f)))
    row = lax.broadcasted_iota(jnp.int32, (ts, ts), 0)
    col = lax.broadcasted_iota(jnp.int32, (ts, ts), 1)
    tri = (col <= row).astype(F32)
    c = jnp.concatenate(
        [jnp.dot(tri[r:r + ts // 2], log_f, precision=lax.Precision.HIGHEST,
                 preferred_element_type=F32) for r in (0, ts // 2)], axis=0) + carry_ref[...]
    carry_ref[...] = c[ts - 1:ts, :]
    c3 = c * inv_scale
    hi = c3.astype(BF16).astype(F32)
    mid = (c3 - hi).astype(BF16).astype(F32)
    lo = (c3 - hi - mid).astype(BF16).astype(F32)
    lane = lax.broadcasted_iota(jnp.int32, (ts, LANES), 1)
    packed = jnp.where(
        lane < n_heads, hi,
        jnp.where(lane < 2 * n_heads, pltpu.roll(mid, n_heads, 1),
                  jnp.where(lane < N_SPLIT * n_heads, pltpu.roll(lo, 2 * n_heads, 1),
                            jnp.where(lane == N_SPLIT * n_heads, 1.0, 0.0))))
    feats = jnp.dot(packed.astype(BF16), place_ref[...], preferred_element_type=F32)
    for h in range(n_heads):
        eq_ref[0, h] = feats[:, h * LANES:(h + 1) * LANES].astype(BF16)
        ek_ref[0, h] = feats[:, (n_heads + h) * LANES:(n_heads + h + 1) * LANES].astype(BF16)


def _placement(n_heads):
    p = np.zeros((LANES, 2 * n_heads * LANES), np.float32)
    one = N_SPLIT * n_heads
    for h in range(n_heads):
        for j in range(N_SPLIT):
            p[j * n_heads + h, h * LANES + j] = 1.0
            p[j * n_heads + h, (n_heads + h) * LANES + N_SPLIT + j] = -1.0
            p[one, h * LANES + N_SPLIT + j] = 1.0
            p[one, (n_heads + h) * LANES + j] = 1.0
    return jnp.asarray(p, BF16)


def _cumlog(h1, w_f, b_f, batch, seq, n_heads, dh, ts=512, side=()):
    assert (N_SPLIT * n_heads) < LANES
    d = h1.shape[1]
    nt = seq // ts
    feat = pl.BlockSpec((1, n_heads, ts, LANES), lambda b, t: (b, 0, t, 0))
    side_in, side_out, side_shapes = _side_specs(side, batch * nt, lambda b, t: b * nt + t)
    return pl.pallas_call(
        functools.partial(_cumlog_kernel, n_side=len(side), ts=ts, n_heads=n_heads,
                          inv_scale=dh ** 0.5),
        grid=(batch, nt),
        in_specs=[pl.BlockSpec((ts, d), lambda b, t: (b * nt + t, 0)),
                  pl.BlockSpec((LANES, d), lambda b, t: (0, 0)),
                  pl.BlockSpec((1, LANES), lambda b, t: (0, 0)),
                  pl.BlockSpec((LANES, 2 * n_heads * LANES), lambda b, t: (0, 0))] + side_in,
        out_specs=[feat, feat] + side_out,
        out_shape=[jax.ShapeDtypeStruct((batch, n_heads, seq, LANES), BF16)] * 2 + side_shapes,
        scratch_shapes=[pltpu.VMEM((1, LANES), F32)],
        compiler_params=_cparams(("parallel", "arbitrary"), 48),
        name="fox_cumlog",
    )(h1, w_f, b_f, _placement(n_heads), *[_side_window(w)[0] for w in side])


FOX_BQ = 512
FOX_BK = 2 * FOX_BQ


def _fox_kernel(q_ref, k_ref, v_ref, eq_ref, ek_ref, *rest, n_side, n_pairs, qk_scale):
    bq, bk = FOX_BQ, FOX_BK
    dh = q_ref.shape[2]
    side_in, o_ref = rest[:n_side], rest[n_side]
    side_out, vt_ref = rest[n_side + 1:2 * n_side + 1], rest[2 * n_side + 1]
    for src_ref, dst_ref in zip(side_in, side_out):
        dst_ref[...] = src_ref[...].astype(dst_ref.dtype)

    for c in range(vt_ref.shape[0]):
        vt_ref[c] = v_ref[0, c * bq:(c + 1) * bq, :].T

    def scores(q, c0, q0, width, masked):
        k0 = c0 * bq
        k = jnp.concatenate([k_ref[0, k0:k0 + width, :], ek_ref[0, 0, k0:k0 + width, :]], axis=1)
        s = lax.dot_general(k, q, (((1,), (1,)), ((), ())),
                            preferred_element_type=F32)
        if masked:
            key = k0 + lax.broadcasted_iota(jnp.int32, (width, bq), 0)
            qry = q0 + lax.broadcasted_iota(jnp.int32, (width, bq), 1)
            s = jnp.where(key <= qry, s, -jnp.inf)
        return s

    def fold(s, carry, c0):
        m, l, acc = carry
        width = s.shape[0]
        m_new = jnp.maximum(m, jnp.max(s, axis=0, keepdims=True))
        alpha = jnp.exp2((m - m_new) * qk_scale)
        p = jnp.exp2((s - m_new) * qk_scale)
        l = alpha * l + jnp.sum(p, axis=0, keepdims=True)
        p = p.astype(BF16)
        pv = jnp.dot(vt_ref[c0], p[:bq, :], preferred_element_type=F32)
        for t in range(1, width // bq):
            pv = pv + jnp.dot(vt_ref[c0 + t], p[t * bq:(t + 1) * bq, :],
                              preferred_element_type=F32)
        return m_new, l, alpha * acc + pv

    tiles = []
    for mi in range(n_pairs):
        tiles += [(2 * mi, 2 * mi, bq, True), (2 * mi + 1, 2 * mi, bk, True)]
        for t in range(mi):
            tiles += [(2 * mi, 2 * t, bk, False), (2 * mi + 1, 2 * t, bk, False)]
    remaining = {}
    for blk, _, _, _ in tiles:
        remaining[blk] = remaining.get(blk, 0) + 1

    q_aug, state = {}, {}

    def issue(tile):
        blk, c0, width, masked = tile
        if blk not in q_aug:
            q0 = blk * bq
            q_aug[blk] = jnp.concatenate([q_ref[0, q0:q0 + bq, :], eq_ref[0, 0, q0:q0 + bq, :]],
                                         axis=1)
            state[blk] = (jnp.full((1, bq), -jnp.inf, F32), jnp.zeros((1, bq), F32),
                          jnp.zeros((dh, bq), F32))
        return scores(q_aug[blk], c0, blk * bq, width, masked)

    def retire(tile, s):
        blk, c0, _, _ = tile
        state[blk] = fold(s, state[blk], c0)
        remaining[blk] -= 1
        if remaining[blk] == 0:
            _, l, acc = state.pop(blk)
            q_aug.pop(blk)
            o_ref[blk * bq:(blk + 1) * bq, :] = (acc / l).T.astype(o_ref.dtype)

    pending = None
    for tile in tiles:
        s = issue(tile)
        if pending is not None:
            retire(*pending)
        pending = (tile, s)
    retire(*pending)


def _fox_attention(qkv, eq, ek, batch, seq, n_heads, dh, side=()):
    assert dh == LANES and seq % FOX_BK == 0 and qkv.shape == (3 * n_heads, batch * seq, dh)
    nq = seq // FOX_BQ
    kern = functools.partial(_fox_kernel, n_side=len(side), n_pairs=nq // 2,
                             qk_scale=dh ** -0.5 * LOG2E)
    feat = pl.BlockSpec((1, 1, seq, LANES), lambda b, h: (b, h, 0, 0))
    side_in, side_out, side_shapes = _side_specs(side, batch * n_heads,
                                                 lambda b, h: b * n_heads + h)
    outs = pl.pallas_call(
        kern,
        grid=(batch, n_heads),
        in_specs=[pl.BlockSpec((1, seq, dh), lambda b, h: (h, b, 0)),
                  pl.BlockSpec((1, seq, dh), lambda b, h: (n_heads + h, b, 0)),
                  pl.BlockSpec((1, seq, dh), lambda b, h: (2 * n_heads + h, b, 0)),
                  feat, feat] + side_in,
        out_specs=[pl.BlockSpec((seq, dh), lambda b, h: (b, h))] + side_out,
        out_shape=[jax.ShapeDtypeStruct((batch * seq, n_heads * dh), BF16)] + side_shapes,
        scratch_shapes=[pltpu.VMEM((nq, dh, FOX_BQ), BF16)],
        compiler_params=_cparams(("parallel", "arbitrary"), 48),
        name="fox_attention",
    )(qkv, qkv, qkv, eq, ek, *[_side_window(w)[0] for w in side])
    return outs if side else outs[0]


def _t5_bucket(dist, num_buckets):
    max_exact = num_buckets // 2
    small = dist < max_exact
    large = max_exact + (np.log(np.maximum(dist, 1) / max_exact) / np.log(MAX_DISTANCE / max_exact)
                         * (num_buckets - max_exact)).astype(np.int64)
    large = np.minimum(large, num_buckets - 1)
    return np.where(small, dist, large)


def _bias_table_kernel(bucket_ref, rb_ref, o_ref, *, num_buckets):
    first = pl.program_id(0) == 0
    bkt = bucket_ref[...]
    key = lax.broadcasted_iota(jnp.int32, bkt.shape, 0)
    qry = lax.broadcasted_iota(jnp.int32, bkt.shape, 1)
    dist = qry + BLOCK - key
    visible = (dist >= 0) & (dist < WINDOW) & ((key >= BLOCK) | jnp.logical_not(first))
    heads = o_ref.shape[1]
    for t in range(heads):
        h = pl.program_id(1) * heads + t
        bias = jnp.zeros(bkt.shape, F32)
        for b in range(num_buckets):
            bias = jnp.where(bkt == b, rb_ref[b, h], bias)
        o_ref[0, t] = jnp.where(visible, bias * LOG2E, -jnp.inf)


BIAS_HEADS_PER_STEP = 8


def _bias_table(rel_bias, n_heads):
    assert n_heads % BIAS_HEADS_PER_STEP == 0
    num_buckets = rel_bias.shape[0]
    ql = np.arange(BLOCK)[None, :]
    kl = np.arange(2 * BLOCK)[:, None]
    bucket = _t5_bucket(np.clip(ql + BLOCK - kl, 0, None), num_buckets).astype(np.int32)
    return pl.pallas_call(
        functools.partial(_bias_table_kernel, num_buckets=num_buckets),
        grid=(2, n_heads // BIAS_HEADS_PER_STEP),
        in_specs=[pl.BlockSpec((2 * BLOCK, BLOCK), lambda f, h: (0, 0)),
                  pl.BlockSpec(memory_space=pltpu.SMEM)],
        out_specs=pl.BlockSpec((1, BIAS_HEADS_PER_STEP, 2 * BLOCK, BLOCK),
                               lambda f, h: (f, h, 0, 0)),
        out_shape=jax.ShapeDtypeStruct((2, n_heads, 2 * BLOCK, BLOCK), F32),
        compiler_params=_cparams(("arbitrary", "arbitrary"), 32),
        name="swa_bias_table",
    )(jnp.asarray(bucket), rel_bias.astype(F32))


def _swa_kernel(q_ref, kp_ref, kc_ref, vp_ref, vc_ref, bias_ref, sink_ref, o_ref, *,
                n_kv, groups, dh, qk_scale):
    qt = q_ref[...].T
    k_band = jnp.concatenate([kp_ref[...], kc_ref[...]], axis=0)
    vt = jnp.concatenate([vp_ref[...], vc_ref[...]], axis=0).T
    outs = []
    for hk in range(n_kv):
        k = k_band[:, hk * dh:(hk + 1) * dh]
        q_grp = jnp.concatenate(
            [qt[(hk * groups + g) * dh:(hk * groups + g + 1) * dh, :] for g in range(groups)],
            axis=1)
        s = jnp.dot(k, q_grp, preferred_element_type=F32) * qk_scale
        probs, denoms = [], []
        for g in range(groups):
            hq = hk * groups + g
            sg = s[:, g * BLOCK:(g + 1) * BLOCK] + bias_ref[0, hq]
            sink = jnp.full((1, BLOCK), sink_ref[hq], F32) * LOG2E
            m = jnp.maximum(jnp.max(sg, axis=0, keepdims=True), sink)
            p = jnp.exp2(sg - m)
            denoms.append(jnp.sum(p, axis=0, keepdims=True) + jnp.exp2(sink - m))
            probs.append(p.astype(BF16))
        o = jnp.dot(vt[hk * dh:(hk + 1) * dh, :], jnp.concatenate(probs, axis=1),
                    preferred_element_type=F32)
        for g in range(groups):
            outs.append((o[:, g * BLOCK:(g + 1) * BLOCK] / denoms[g]).astype(o_ref.dtype))
    o_ref[...] = jnp.concatenate(outs, axis=0).T


def _swa_attention(qkv, bias, sinks, batch, seq, n_heads, n_kv, dh, q_col0):
    nb = seq // BLOCK
    qw = n_heads * dh
    kw = n_kv * dh
    assert q_col0 % qw == 0 and (q_col0 + qw) % kw == 0
    qb = q_col0 // qw
    kb = (q_col0 + qw) // kw
    vb = kb + 1
    prev = lambda b, i: b * nb + jnp.maximum(i - 1, 0)
    cur = lambda b, i: b * nb + i
    kern = functools.partial(_swa_kernel, n_kv=n_kv, groups=n_heads // n_kv, dh=dh,
                             qk_scale=dh ** -0.5 * LOG2E)
    return pl.pallas_call(
        kern,
        grid=(batch, nb),
        in_specs=[pl.BlockSpec((BLOCK, qw), lambda b, i: (cur(b, i), qb)),
                  pl.BlockSpec((BLOCK, kw), lambda b, i: (prev(b, i), kb)),
                  pl.BlockSpec((BLOCK, kw), lambda b, i: (cur(b, i), kb)),
                  pl.BlockSpec((BLOCK, kw), lambda b, i: (prev(b, i), vb)),
                  pl.BlockSpec((BLOCK, kw), lambda b, i: (cur(b, i), vb)),
                  pl.BlockSpec((1, n_heads, 2 * BLOCK, BLOCK),
                               lambda b, i: (jnp.minimum(i, 1), 0, 0, 0)),
                  pl.BlockSpec(memory_space=pltpu.SMEM)],
        out_specs=pl.BlockSpec((BLOCK, qw), lambda b, i: (cur(b, i), 0)),
        out_shape=jax.ShapeDtypeStruct((batch * seq, qw), BF16),
        compiler_params=_cparams(("parallel", "arbitrary"), 40),
        name="swa_attention",
    )(qkv, qkv, qkv, qkv, qkv, bias, sinks.astype(F32))


def kernel(x, norm1_g, w_in, b_forget, attn_sinks, rel_bias, w_branch_a, w_branch_b, w_out,
           norm2_g, w_ffn_gate, w_ffn_up, w_ffn_down, final_g):
    batch, seq, d = x.shape
    depth = w_in.shape[0]
    n_ha = b_forget.shape[1]
    n_hb = attn_sinks.shape[1]
    wa = w_branch_a.shape[1]
    wb = w_branch_b.shape[1]
    dh_a = wa // n_ha
    dh_b = wb // n_hb
    w_kvb = (w_in.shape[2] - 3 * wa - n_ha - wb - 2 * d) // 2
    n_kvb = w_kvb // dh_b
    c_f = 3 * wa
    c_qb = c_f + n_ha
    c_ga = c_qb + wb + 2 * w_kvb
    c_gb = c_ga + d

    xf = x.reshape(batch * seq, d)
    bias_tab = _bias_table(rel_bias, n_hb)

    for l in range(depth):
        wt = jnp.transpose(w_in[l])
        head_rows = max(r for r in range(BF16_ROWS, CAST_MAX_ROWS + 1, BF16_ROWS) if c_qb % r == 0)
        wt_head = _cast_rows_bf16(wt, 0, c_qb, head_rows, "cast_w_in_head")
        w_f = jnp.pad(wt_head[c_f:], ((0, LANES - n_ha), (0, 0)))
        b_f = jnp.pad(b_forget[l].astype(F32), (0, LANES - n_ha)).reshape(1, LANES)

        h1 = _rmsnorm(xf, norm1_g[l], BF16, "rmsnorm1")
        qkv_a, wt_tail = _matmul(
            [h1], [(wt_head, 0, True)], [(0, 0)], [], _ep_identity, BF16, 2048, 512, n=c_f,
            head_dim=dh_a, side=((wt, c_qb, wt.shape[0] - c_qb),), vmem_mib=58, name="in_proj_fox")
        w_ga = (wt_tail, c_ga - c_qb, True)
        w_gb = (wt_tail, c_gb - c_qb, True)
        qkv_b = _matmul([h1], [(wt_tail, 0, True)], [(0, 0)], [], _ep_identity, BF16, 2048, 512,
                        n=c_ga - c_qb, vmem_mib=56, name="in_proj_swa")
        eq, ek = _cumlog(h1, w_f, b_f, batch, seq, n_ha, dh_a)
        oa, w_out_b, w_bra, w_brb = _fox_attention(
            qkv_a, eq, ek, batch, seq, n_ha, dh_a, side=(w_out[l], w_branch_a[l], w_branch_b[l]))
        ob = _swa_attention(qkv_b, bias_tab, attn_sinks[l], batch, seq, n_hb, n_kvb, dh_b, 0)
        mixed, w_gate_b, w_up_b = _matmul(
            [h1, oa, ob], [w_ga, w_gb, w_bra, w_brb],
            [(0, 0), (0, 1), (1, 2), (2, 3)], [], _ep_gated_merge, BF16, 1024, 256, n=d,
            side=(w_ffn_gate[l], w_ffn_up[l]), row_chunk=MM_ROW_CHUNK, vmem_mib=60,
            name="gated_merge")
        xf = _matmul([mixed], [w_out_b], [(0, 0)], [xf], _ep_residual, F32,
                     1024, 1024, row_chunk=MM_ROW_CHUNK, vmem_mib=58, name="out_proj")

        h2 = _rmsnorm(xf, norm2_g[l], BF16, "rmsnorm2")
        hidden, w_down_b = _matmul([h2], [w_gate_b, w_up_b], [(0, 0), (0, 1)], [], _ep_swiglu,
                                   BF16, 2048, 256, side=(w_ffn_down[l],),
                                   row_chunk=MM_ROW_CHUNK, vmem_mib=56, name="ffn_gate_up")
        xf = _matmul([hidden], [w_down_b], [(0, 0)], [xf], _ep_residual, F32,
                     512, 1024, n_outer=True, vmem_mib=60, name="ffn_down")

    out = _rmsnorm(xf, final_g, x.dtype, "rmsnorm_final")
    return out.reshape(batch, seq, d)
```
